```python
import math
import jax, jax.numpy as jnp
from jax import lax
import numpy as np

D_MODEL = 1024
BATCH = 8
SEQ = 4096
DEPTH = 2
DEC_BATCH = 2
DEC_SEQ = 8192
PAST_LEN = 128

N_MIXERS = 2
N_ATTN_LAYERS = (DEPTH + 1) // 2
N_FNET_LAYERS = DEPTH // 2
N_HEADS = 8
HEAD_DIM = 64
Q_BLOCK = 128
REL_BUCKETS = 32
REL_MAX_DIST = 128
FNET_GROUPS = 4
FNET_GROUP_DIM = D_MODEL // FNET_GROUPS
N_GROUPS = 4
EXPERTS_PER_GROUP = 8
N_EXPERTS = N_GROUPS * EXPERTS_PER_GROUP
TOP_K = 2
D_EXPERT = 512
ROW_BLOCK = 128
ALPHA = (2 * DEPTH) ** 0.25
BETA = (8 * DEPTH) ** -0.25
LN_EPS = 1e-5

kernel_name = "diffattn_fnet_hmoe_deepnorm_encoder"


def layer_norm(x, g, b):
    xf = x.astype(jnp.float32)
    mu = jnp.mean(xf, axis=-1, keepdims=True)
    var = jnp.mean(jnp.square(xf - mu), axis=-1, keepdims=True)
    y = (xf - mu) * lax.rsqrt(var + LN_EPS) * g.astype(jnp.float32) + b.astype(jnp.float32)
    return y.astype(x.dtype)


def rel_bucket(rel):
    nb = REL_BUCKETS // 2
    max_exact = nb // 2
    ret = jnp.where(rel > 0, nb, 0)
    n = jnp.abs(rel)
    nf = jnp.maximum(n, 1).astype(jnp.float32)
    large = max_exact + (jnp.log(nf / max_exact) / math.log(REL_MAX_DIST / max_exact)
                         * (nb - max_exact)).astype(jnp.int32)
    large = jnp.minimum(large, nb - 1)
    return ret + jnp.where(n < max_exact, n, large)


def diff_attention(x, rel_bias, w_qkv, lq1, lk1, lq2, lk2, subln_g, w_o, lam_init):
    B_, S_, _ = x.shape
    qkv = x @ w_qkv
    q = qkv[..., :D_MODEL].reshape(B_, S_, N_HEADS, 2, HEAD_DIM)
    k = qkv[..., D_MODEL:2 * D_MODEL].reshape(B_, S_, N_HEADS, 2, HEAD_DIM)
    v = qkv[..., 2 * D_MODEL:].reshape(B_, S_, N_HEADS, 2 * HEAD_DIM)
    f32 = jnp.float32
    lam = (jnp.exp(jnp.sum(lq1.astype(f32) * lk1.astype(f32)))
           - jnp.exp(jnp.sum(lq2.astype(f32) * lk2.astype(f32))) + lam_init)
    nb = S_ // Q_BLOCK
    qb = jnp.moveaxis(q.reshape(B_, nb, Q_BLOCK, N_HEADS, 2, HEAD_DIM), 1, 0)
    kpos = jnp.arange(S_, dtype=jnp.int32)
    scale = HEAD_DIM ** -0.5

    def block(args):
        qblk, bi = args
        qpos = bi * Q_BLOCK + jnp.arange(Q_BLOCK, dtype=jnp.int32)
        bias = rel_bias[rel_bucket(kpos[None, :] - qpos[:, None])].astype(f32)
        bias = jnp.transpose(bias, (2, 0, 1))
        s = jnp.einsum('bqhcd,bkhcd->bhcqk', qblk, k).astype(f32) * scale + bias[None, :, None]
        p = jax.nn.softmax(s, axis=-1)
        a = p[:, :, 0] - lam * p[:, :, 1]
        return jnp.einsum('bhqk,bkhe->bqhe', a.astype(v.dtype), v)

    o = lax.map(block, (qb, jnp.arange(nb, dtype=jnp.int32)))
    o = jnp.moveaxis(o, 0, 1).reshape(B_, S_, N_HEADS, 2 * HEAD_DIM).astype(f32)
    o = (o * lax.rsqrt(jnp.mean(jnp.square(o), axis=-1, keepdims=True) + LN_EPS)
         * subln_g.astype(f32) * (1.0 - lam_init))
    return o.astype(x.dtype).reshape(B_, S_, D_MODEL) @ w_o


def fourier_mix(x, w_o, b_o):
    B_, S_, _ = x.shape
    xg = x.astype(jnp.float32).reshape(B_, S_, FNET_GROUPS, FNET_GROUP_DIM)
    f = jnp.fft.fft2(xg, axes=(1, 3), norm='ortho').real
    return f.astype(x.dtype).reshape(B_, S_, D_MODEL) @ w_o + b_o


def hier_moe(x, w_group, b_group, w_fine, b_fine, w1, w3, w2):
    B_, S_, D_ = x.shape
    T = B_ * S_
    xt = x.reshape(T, D_)
    f32 = jnp.float32
    pg = jax.nn.softmax((xt @ w_group + b_group).astype(f32), axis=-1)
    pg_top, g_idx = lax.top_k(pg, 1)
    lf = (xt @ w_fine + b_fine).astype(f32).reshape(T, N_GROUPS, EXPERTS_PER_GROUP)
    sel = jnp.broadcast_to(g_idx[:, :, None], (T, 1, EXPERTS_PER_GROUP))
    lf_sel = jnp.take_along_axis(lf, sel, axis=1)[:, 0]
    fv, fi = lax.top_k(lf_sel, TOP_K)
    gate = (pg_top * jax.nn.softmax(fv, axis=-1)).reshape(-1)
    eid = (g_idx * EXPERTS_PER_GROUP + fi).reshape(-1).astype(jnp.int32)
    A = T * TOP_K
    tok = jnp.repeat(jnp.arange(T, dtype=jnp.int32), TOP_K)
    order = jnp.argsort(eid)
    eid_s = eid[order]
    counts = jnp.zeros((N_EXPERTS,), jnp.int32).at[eid].add(1)
    padded = (counts + ROW_BLOCK - 1) // ROW_BLOCK * ROW_BLOCK
    starts = jnp.cumsum(counts) - counts
    pends = jnp.cumsum(padded)
    pstarts = pends - padded
    dest = pstarts[eid_s] + (jnp.arange(A, dtype=jnp.int32) - starts[eid_s])
    NB = -(-A // ROW_BLOCK) + N_EXPERTS
    P = NB * ROW_BLOCK
    row_tok = jnp.full((P,), T, jnp.int32).at[dest].set(tok[order])
    row_gate = jnp.zeros((P,), f32).at[dest].set(gate[order])
    block_exp = jnp.minimum(
        jnp.searchsorted(pends, jnp.arange(NB, dtype=jnp.int32) * ROW_BLOCK, side='right'),
        N_EXPERTS - 1).astype(jnp.int32)
    xpad = jnp.concatenate([xt, jnp.zeros((1, D_), xt.dtype)], axis=0)
    xb = xpad[row_tok].reshape(NB, ROW_BLOCK, D_)

    def expert_block(args):
        xblk, e = args
        h = jax.nn.silu(xblk @ w1[e]) * (xblk @ w3[e])
        return h @ w2[e]

    yb = lax.map(expert_block, (xb, block_exp)).reshape(P, D_)
    yb = yb * row_gate[:, None].astype(yb.dtype)
    out = jnp.zeros((T + 1, D_), yb.dtype).at[row_tok].add(yb)[:T]
    return out.reshape(B_, S_, D_)


def trunk(x, rel_bias, attn_w_qkv, attn_lambda_q1, attn_lambda_k1, attn_lambda_q2, attn_lambda_k2,
          attn_subln_g, attn_w_o, fnet_w_o, fnet_b_o, ln1_g, ln1_b, ln2_g, ln2_b,
          moe_w_group, moe_b_group, moe_w_fine, moe_b_fine, moe_w1, moe_w3, moe_w2):
    for i in range(DEPTH):
        j = i // N_MIXERS
        if i % N_MIXERS == 0:
            lam_init = 0.8 - 0.6 * math.exp(-0.3 * i)
            h = diff_attention(x, rel_bias, attn_w_qkv[j], attn_lambda_q1[j], attn_lambda_k1[j],
                               attn_lambda_q2[j], attn_lambda_k2[j], attn_subln_g[j], attn_w_o[j],
                               lam_init)
        else:
            h = fourier_mix(x, fnet_w_o[j], fnet_b_o[j])
        x = layer_norm(ALPHA * x + h, ln1_g[i], ln1_b[i])
        m = hier_moe(x, moe_w_group[i], moe_b_group[i], moe_w_fine[i], moe_b_fine[i],
                     moe_w1[i], moe_w3[i], moe_w2[i])
        x = layer_norm(ALPHA * x + m, ln2_g[i], ln2_b[i])
    return x


def setup_inputs(seed: int = 0) -> dict:
    key = jax.random.key(seed)
    ks = jax.random.split(key, 24)
    nrm = jax.random.normal
    f32 = jnp.float32
    D = D_MODEL
    w_qk = nrm(ks[2], (N_ATTN_LAYERS, D, 2 * D), f32) * D ** -0.5
    w_v = nrm(ks[3], (N_ATTN_LAYERS, D, D), f32) * D ** -0.5 * BETA
    return {
        "x_prompt": nrm(ks[0], (BATCH, SEQ, D), f32),
        "x_sample": nrm(ks[1], (DEC_BATCH, DEC_SEQ, D), f32),
        "rel_bias": nrm(ks[4], (REL_BUCKETS, N_HEADS), f32) * 0.5,
        "attn_w_qkv": jnp.concatenate([w_qk, w_v], axis=-1),
        "attn_lambda_q1": nrm(ks[5], (N_ATTN_LAYERS, HEAD_DIM), f32) * 0.1,
        "attn_lambda_k1": nrm(ks[6], (N_ATTN_LAYERS, HEAD_DIM), f32) * 0.1,
        "attn_lambda_q2": nrm(ks[7], (N_ATTN_LAYERS, HEAD_DIM), f32) * 0.1,
        "attn_lambda_k2": nrm(ks[8], (N_ATTN_LAYERS, HEAD_DIM), f32) * 0.1,
        "attn_subln_g": 1.0 + 0.02 * nrm(ks[9], (N_ATTN_LAYERS, 2 * HEAD_DIM), f32),
        "attn_w_o": nrm(ks[10], (N_ATTN_LAYERS, D, D), f32) * D ** -0.5 * BETA,
        "fnet_w_o": nrm(ks[11], (N_FNET_LAYERS, D, D), f32) * D ** -0.5 * BETA,
        "fnet_b_o": 0.02 * nrm(ks[12], (N_FNET_LAYERS, D), f32),
        "ln1_g": 1.0 + 0.02 * nrm(ks[13], (DEPTH, D), f32),
        "ln1_b": 0.02 * nrm(ks[14], (DEPTH, D), f32),
        "ln2_g": 1.0 + 0.02 * nrm(ks[15], (DEPTH, D), f32),
        "ln2_b": 0.02 * nrm(ks[16], (DEPTH, D), f32),
        "moe_w_group": nrm(ks[17], (DEPTH, D, N_GROUPS), f32) * D ** -0.5,
        "moe_b_group": 0.01 * nrm(ks[18], (DEPTH, N_GROUPS), f32),
        "moe_w_fine": nrm(ks[19], (DEPTH, D, N_EXPERTS), f32) * D ** -0.5,
        "moe_b_fine": 0.01 * nrm(ks[20], (DEPTH, N_EXPERTS), f32),
        "moe_w1": nrm(ks[21], (DEPTH, N_EXPERTS, D, D_EXPERT), f32) * D ** -0.5,
        "moe_w3": nrm(ks[22], (DEPTH, N_EXPERTS, D, D_EXPERT), f32) * D ** -0.5,
        "moe_w2": nrm(ks[23], (DEPTH, N_EXPERTS, D_EXPERT, D), f32) * D_EXPERT ** -0.5 * BETA,
    }


def reference(x_prompt, x_sample, rel_bias, attn_w_qkv, attn_lambda_q1, attn_lambda_k1,
              attn_lambda_q2, attn_lambda_k2, attn_subln_g, attn_w_o, fnet_w_o, fnet_b_o,
              ln1_g, ln1_b, ln2_g, ln2_b, moe_w_group, moe_b_group, moe_w_fine, moe_b_fine,
              moe_w1, moe_w3, moe_w2):
    y_prompt = trunk(x_prompt, rel_bias, attn_w_qkv, attn_lambda_q1, attn_lambda_k1, attn_lambda_q2,
                     attn_lambda_k2, attn_subln_g, attn_w_o, fnet_w_o, fnet_b_o, ln1_g, ln1_b,
                     ln2_g, ln2_b, moe_w_group, moe_b_group, moe_w_fine, moe_b_fine,
                     moe_w1, moe_w3, moe_w2)
    y_sample = trunk(x_sample, rel_bias, attn_w_qkv, attn_lambda_q1, attn_lambda_k1, attn_lambda_q2,
                     attn_lambda_k2, attn_subln_g, attn_w_o, fnet_w_o, fnet_b_o, ln1_g, ln1_b,
                     ln2_g, ln2_b, moe_w_group, moe_b_group, moe_w_fine, moe_b_fine,
                     moe_w1, moe_w3, moe_w2)
    return (y_prompt, y_sample)
```

```python
import functools
import math

import numpy as np
import jax
import jax.numpy as jnp
from jax import lax
from jax.experimental import pallas as pl
from jax.experimental.pallas import tpu as pltpu

F32 = jnp.float32
BF16 = jnp.bfloat16

HEAD_DIM = 64
HEAD_W = 2 * HEAD_DIM
REL_BUCKETS = 32
REL_MAX_DIST = 128
FNET_GROUPS = 4
N_GROUPS = 4
EXPERTS_PER_GROUP = 8
N_EXPERTS = N_GROUPS * EXPERTS_PER_GROUP
DEPTH = 2
ALPHA = (2 * DEPTH) ** 0.25
LN_EPS = 1e-5
LOG2E = 1.4426950408889634

LANES = 128
SUBLANES = 8
VMEM_LIMIT_BYTES = 52 * 1024 * 1024

ROW_TILE = 512
ATT_TILE = 512
DMA_TILE = 256
EXPERT_BLOCK = 256
FFT_S1 = 64
FFT_KB = SUBLANES
FFT_COLS = 8192
ROUTER_ROWS = 48
FINE_ROW0 = 8
NEG_BIG = -1e30


def _cparams(*sem):
    return pltpu.CompilerParams(dimension_semantics=sem, vmem_limit_bytes=VMEM_LIMIT_BYTES)


def _split_bf16(a):
    hi = a.astype(BF16)
    lo = (a - hi.astype(F32)).astype(BF16)
    return hi, lo


def _qkv_kernel(x_ref, w_ref, o_ref, *, d):
    xb = x_ref[...].astype(BF16)
    for c in range(w_ref.shape[1] // d):
        o_ref[:, c * d:(c + 1) * d] = jnp.dot(
            xb, w_ref[:, c * d:(c + 1) * d], preferred_element_type=F32).astype(BF16)


def _qkv_proj(x2, w_bf16):
    t, d = x2.shape
    n = w_bf16.shape[1]
    return pl.pallas_call(
        functools.partial(_qkv_kernel, d=d),
        out_shape=jax.ShapeDtypeStruct((t, n), BF16),
        grid=(t // ROW_TILE,),
        in_specs=[pl.BlockSpec((ROW_TILE, d), lambda i: (i, 0)),
                  pl.BlockSpec((d, n), lambda i: (0, 0))],
        out_specs=pl.BlockSpec((ROW_TILE, n), lambda i: (i, 0)),
        compiler_params=_cparams("parallel"),
        name="qkv_proj",
    )(x2, w_bf16)


def _attn_kernel(lam_ref, q_ref, k_ref, v_ref, bias_ref, g_ref, o_ref,
                 qbd_ref, m_ref, l_ref, acc_ref, *, tile, nk, out_scale):
    qi = pl.program_id(2)
    qf = q_ref[...].astype(F32)
    lane = lax.broadcasted_iota(jnp.int32, qf.shape, 1)
    qbd_ref[0:tile, :] = jnp.where(lane < HEAD_DIM, qf, 0.0).astype(BF16)
    qbd_ref[tile:2 * tile, :] = jnp.where(lane >= HEAD_DIM, qf, 0.0).astype(BF16)
    m_ref[...] = jnp.full(m_ref.shape, NEG_BIG, F32)
    l_ref[...] = jnp.zeros(l_ref.shape, F32)
    acc_ref[...] = jnp.zeros(acc_ref.shape, F32)

    def body(kb, carry):
        off = pl.multiple_of(kb * tile, tile)
        k = k_ref[pl.ds(off, tile), :]
        v = v_ref[pl.ds(off, tile), :]
        b = bias_ref[jnp.clip(kb - qi, -2, 2) + 2]
        s = lax.dot_general(k, qbd_ref[...], (((1,), (1,)), ((), ())), preferred_element_type=F32)
        s = s + jnp.concatenate([b, b], axis=1)
        m_old = m_ref[...]
        m_new = jnp.maximum(m_old, jnp.max(s, axis=0, keepdims=True))
        alpha = jnp.exp2(m_old - m_new)
        p = jnp.exp2(s - m_new)
        l_ref[...] = alpha * l_ref[...] + jnp.sum(p, axis=0, keepdims=True)
        pv = lax.dot_general(v, p.astype(BF16), (((0,), (0,)), ((), ())), preferred_element_type=F32)
        acc_ref[...] = acc_ref[...] * alpha + pv
        m_ref[...] = m_new
        return carry

    lax.fori_loop(0, nk, body, 0)

    lam = lam_ref[0]
    inv = 1.0 / l_ref[...]
    acc = acc_ref[...]
    o_t = acc[:, :tile] * inv[:, :tile] - lam * (acc[:, tile:] * inv[:, tile:])
    ms = jnp.mean(o_t * o_t, axis=0, keepdims=True)
    o_t = o_t * lax.rsqrt(ms + LN_EPS)
    o_ref[...] = ((o_t.T * g_ref[...]) * out_scale).astype(BF16)


def _attention(qkv, bias_tab, lam, subln_g, b, s, d, lam_init):
    h = d // HEAD_W
    tile = ATT_TILE
    nq = s // tile
    kern = functools.partial(_attn_kernel, tile=tile, nk=nq, out_scale=1.0 - lam_init)
    return pl.pallas_call(
        kern,
        out_shape=jax.ShapeDtypeStruct((b * s, d), BF16),
        grid=(b, h, nq),
        in_specs=[
            pl.BlockSpec(memory_space=pltpu.SMEM),
            pl.BlockSpec((tile, HEAD_W), lambda bi, hi, qi: (bi * nq + qi, hi)),
            pl.BlockSpec((s, HEAD_W), lambda bi, hi, qi: (bi, h + hi)),
            pl.BlockSpec((s, HEAD_W), lambda bi, hi, qi: (bi, 2 * h + hi)),
            pl.BlockSpec((None, 5, tile, tile), lambda bi, hi, qi: (hi, 0, 0, 0)),
            pl.BlockSpec((1, HEAD_W), lambda bi, hi, qi: (0, 0)),
        ],
        out_specs=pl.BlockSpec((tile, HEAD_W), lambda bi, hi, qi: (bi * nq + qi, hi)),
        scratch_shapes=[pltpu.VMEM((2 * tile, HEAD_W), BF16),
                        pltpu.VMEM((1, 2 * tile), F32),
                        pltpu.VMEM((1, 2 * tile), F32),
                        pltpu.VMEM((HEAD_W, 2 * tile), F32)],
        compiler_params=_cparams("parallel", "parallel", "arbitrary"),
        name="diff_attention",
    )(lam, qkv, qkv, qkv, bias_tab, subln_g)


def _rel_bucket(rel):
    nb = REL_BUCKETS // 2
    max_exact = nb // 2
    ret = jnp.where(rel > 0, nb, 0)
    n = jnp.abs(rel)
    nf = jnp.maximum(n, 1).astype(F32)
    large = max_exact + (jnp.log(nf / max_exact) / math.log(REL_MAX_DIST / max_exact)
                         * (nb - max_exact)).astype(jnp.int32)
    large = jnp.minimum(large, nb - 1)
    return ret + jnp.where(n < max_exact, n, large)


def _bias_tiles(rel_bias, tile):
    assert tile >= REL_MAX_DIST
    dd = jnp.arange(-2, 3, dtype=jnp.int32)[:, None, None]
    kk = jnp.arange(tile, dtype=jnp.int32)[None, :, None]
    qq = jnp.arange(tile, dtype=jnp.int32)[None, None, :]
    bucket = _rel_bucket(dd * tile + kk - qq)
    tab = jnp.transpose(rel_bias.astype(F32) * LOG2E)[:, bucket]
    return tab


def _ln(z, g, b):
    mu = jnp.mean(z, axis=-1, keepdims=True)
    zc = z - mu
    var = jnp.mean(zc * zc, axis=-1, keepdims=True)
    return zc * lax.rsqrt(var + LN_EPS) * g + b


def _route_epilogue(xn, wr_hi_ref, wr_lo_ref, br_ref, tri_ref, route_ref, counts_ref, base_ref):
    rows = xn.shape[0]
    x_hi, x_lo = _split_bf16(xn)
    nt = (((1,), (1,)), ((), ()))
    logits = (lax.dot_general(wr_hi_ref[...], x_hi, nt, preferred_element_type=F32)
              + lax.dot_general(wr_hi_ref[...], x_lo, nt, preferred_element_type=F32)
              + lax.dot_general(wr_lo_ref[...], x_hi, nt, preferred_element_type=F32))
    logits = logits + br_ref[...]

    lg = logits[0:FINE_ROW0]
    gmax = jnp.max(lg, axis=0, keepdims=True)
    pg_top = 1.0 / jnp.sum(jnp.exp(lg - gmax), axis=0, keepdims=True)
    gio = lax.broadcasted_iota(jnp.int32, lg.shape, 0).astype(F32)
    g_idx = jnp.min(jnp.where(lg == gmax, gio, float(FINE_ROW0)), axis=0, keepdims=True)

    sel = jnp.zeros((EXPERTS_PER_GROUP, rows), F32)
    for g in range(N_GROUPS):
        r0 = FINE_ROW0 + g * EXPERTS_PER_GROUP
        sel = jnp.where(g_idx == float(g), logits[r0:r0 + EXPERTS_PER_GROUP], sel)
    eio = lax.broadcasted_iota(jnp.int32, sel.shape, 0).astype(F32)
    v1 = jnp.max(sel, axis=0, keepdims=True)
    i1 = jnp.min(jnp.where(sel == v1, eio, float(EXPERTS_PER_GROUP)), axis=0, keepdims=True)
    rest = jnp.where(eio == i1, -jnp.inf, sel)
    v2 = jnp.max(rest, axis=0, keepdims=True)
    i2 = jnp.min(jnp.where(rest == v2, eio, float(EXPERTS_PER_GROUP)), axis=0, keepdims=True)
    e2 = jnp.exp(v2 - v1)
    den = 1.0 + e2
    gate1 = pg_top / den
    gate2 = pg_top * e2 / den
    eid1 = g_idx * float(EXPERTS_PER_GROUP) + i1
    eid2 = g_idx * float(EXPERTS_PER_GROUP) + i2

    xio = lax.broadcasted_iota(jnp.int32, (N_EXPERTS, rows), 0).astype(F32)
    oh1 = xio == eid1
    oh2 = xio == eid2
    cnt = jnp.where(oh1 | oh2, 1.0, 0.0)
    incl = jnp.dot(cnt.astype(BF16), tri_ref[...], preferred_element_type=F32)
    before = incl - cnt + base_ref[...]
    rank1 = jnp.sum(jnp.where(oh1, before, 0.0), axis=0, keepdims=True)
    rank2 = jnp.sum(jnp.where(oh2, before, 0.0), axis=0, keepdims=True)
    base_new = base_ref[...] + jnp.sum(cnt, axis=1, keepdims=True)
    base_ref[...] = base_new
    counts_ref[...] = jnp.broadcast_to(base_new, counts_ref.shape)
    rio = lax.broadcasted_iota(jnp.int32, route_ref.shape, 0)
    out = jnp.zeros(route_ref.shape, F32)
    for r, val in enumerate((eid1, eid2, gate1, gate2, rank1, rank2)):
        out = jnp.where(rio == r, val, out)
    route_ref[...] = out


def _proj_ln_kernel(a_ref, w_ref, x_ref, g_ref, b_ref, wr_hi_ref, wr_lo_ref, br_ref, tri_ref,
                    xn_ref, route_ref, counts_ref, base_ref):
    @pl.when(pl.program_id(0) == 0)
    def _():
        base_ref[...] = jnp.zeros(base_ref.shape, F32)

    h = jnp.dot(a_ref[...], w_ref[...], preferred_element_type=F32)
    xn = _ln(ALPHA * x_ref[...] + h, g_ref[...], b_ref[...])
    xn_ref[...] = xn
    _route_epilogue(xn, wr_hi_ref, wr_lo_ref, br_ref, tri_ref, route_ref, counts_ref, base_ref)


def _fft2_ln_kernel(a_ref, gbig_ref, mr_ref, mi_ref, bo_ref, x_ref, g_ref, b_ref,
                    wr_hi_ref, wr_lo_ref, br_ref, tri_ref,
                    xn_ref, route_ref, counts_ref, base_ref):
    @pl.when(pl.program_id(0) == 0)
    def _():
        base_ref[...] = jnp.zeros(base_ref.shape, F32)

    rows = xn_ref.shape[0]
    d = xn_ref.shape[1]
    a = a_ref[...].reshape(2 * rows, d)
    v = jnp.dot(gbig_ref[...], a, preferred_element_type=F32)
    h = (jnp.dot(v[:rows].astype(BF16), mr_ref[...], preferred_element_type=F32)
         + jnp.dot(v[rows:].astype(BF16), mi_ref[...], preferred_element_type=F32)
         + bo_ref[...])
    xn = _ln(ALPHA * x_ref[...].reshape(rows, d) + h, g_ref[...], b_ref[...])
    xn_ref[...] = xn
    _route_epilogue(xn, wr_hi_ref, wr_lo_ref, br_ref, tri_ref, route_ref, counts_ref, base_ref)


def _router_specs(d, rows):
    const2 = lambda i: (0, 0)
    ins = [pl.BlockSpec((1, d), const2), pl.BlockSpec((1, d), const2),
           pl.BlockSpec((ROUTER_ROWS, d), const2), pl.BlockSpec((ROUTER_ROWS, d), const2),
           pl.BlockSpec((ROUTER_ROWS, 1), const2), pl.BlockSpec((rows, rows), const2)]
    outs = [pl.BlockSpec((rows, d), lambda i: (i, 0)),
            pl.BlockSpec((SUBLANES, rows), lambda i: (0, i)),
            pl.BlockSpec((N_EXPERTS, LANES), const2)]
    return ins, outs


def _router_out_shapes(t, d):
    return (jax.ShapeDtypeStruct((t, d), F32),
            jax.ShapeDtypeStruct((SUBLANES, t), F32),
            jax.ShapeDtypeStruct((N_EXPERTS, LANES), F32))


def _proj_ln_router(att, wo_bf16, x2, ln_g, ln_b, router):
    t, d = x2.shape
    rows = ROW_TILE
    tail_in, outs = _router_specs(d, rows)
    return pl.pallas_call(
        _proj_ln_kernel,
        out_shape=_router_out_shapes(t, d),
        grid=(t // rows,),
        in_specs=[pl.BlockSpec((rows, d), lambda i: (i, 0)),
                  pl.BlockSpec((d, d), lambda i: (0, 0)),
                  pl.BlockSpec((rows, d), lambda i: (i, 0))] + tail_in,
        out_specs=outs,
        scratch_shapes=[pltpu.VMEM((N_EXPERTS, 1), F32)],
        compiler_params=_cparams("arbitrary"),
        name="proj_ln_router",
    )(att, wo_bf16, x2, ln_g, ln_b, *router)


def _fft1_kernel(f_ref, x_ref, o_ref):
    o_ref[...] = jnp.dot(f_ref[...], x_ref[...].astype(BF16), preferred_element_type=F32).astype(BF16)


def _fft_stage1(x3):
    b, s, d = x3.shape
    s1 = FFT_S1
    s2 = s // s1
    k = np.arange(s2)
    ang = 2.0 * np.pi * ((k[:, None] * k[None, :]) % s2) / s2
    f = jnp.asarray(np.concatenate([np.cos(ang), -np.sin(ang)], axis=0), dtype=BF16)
    cols = s1 * d
    tn = min(FFT_COLS, cols)
    return pl.pallas_call(
        _fft1_kernel,
        out_shape=jax.ShapeDtypeStruct((b, 2 * s2, cols), BF16),
        grid=(b, cols // tn),
        in_specs=[pl.BlockSpec((2 * s2, s2), lambda bi, ci: (0, 0)),
                  pl.BlockSpec((None, s2, tn), lambda bi, ci: (bi, 0, ci))],
        out_specs=pl.BlockSpec((None, 2 * s2, tn), lambda bi, ci: (bi, 0, ci)),
        compiler_params=_cparams("parallel", "parallel"),
        name="fft_stage1",
    )(f, x3.reshape(b, s2, cols))


def _fft_stage2_matrix(s):
    s1 = FFT_S1
    s2 = s // s1
    kb = FFT_KB
    k2 = np.arange(s2)[:, None, None]
    k1 = np.arange(s1)[None, :, None]
    n1 = np.arange(s1)[None, None, :]
    ang = 2.0 * np.pi * ((k1 * n1 * s2 + k2 * n1) % s) / s
    gr = jnp.asarray(np.cos(ang), dtype=BF16).reshape(s2 // kb, kb, s1, s1)
    gi = jnp.asarray(-np.sin(ang), dtype=BF16).reshape(s2 // kb, kb, s1, s1)
    eye = jnp.eye(kb, dtype=BF16)

    def expand(g):
        return jnp.transpose(g, (0, 2, 1, 3))[:, :, :, None, :] * eye[None, None, :, :, None]

    er, ei = expand(gr), expand(gi)
    top = jnp.stack([er, -ei], axis=3)
    bot = jnp.stack([ei, er], axis=3)
    big = jnp.stack([top, bot], axis=1)
    m = 2 * kb * s1
    return big.reshape(s2 // kb, m, m)


def _fold_kernel(cs_hi_ref, cs_lo_ref, w_ref, o_ref):
    w_hi, w_lo = _split_bf16(w_ref[...])
    o_ref[...] = (jnp.dot(cs_hi_ref[...], w_hi, preferred_element_type=F32)
                  + jnp.dot(cs_hi_ref[...], w_lo, preferred_element_type=F32)
                  + jnp.dot(cs_lo_ref[...], w_hi, preferred_element_type=F32)).astype(BF16)


def _fold_channel_dft(w_o, s):
    d = w_o.shape[0]
    cg = d // FNET_GROUPS
    c = np.arange(cg)
    ang = 2.0 * np.pi * ((c[:, None] * c[None, :]) % cg) / cg
    scale = 1.0 / math.sqrt(s * cg)
    cs = jnp.asarray(np.stack([np.cos(ang), np.sin(ang)]) * scale, dtype=F32)
    cs_hi, cs_lo = _split_bf16(cs)
    out = pl.pallas_call(
        _fold_kernel,
        out_shape=jax.ShapeDtypeStruct((2, d, d), BF16),
        grid=(2, FNET_GROUPS),
        in_specs=[pl.BlockSpec((None, cg, cg), lambda ci, gi: (ci, 0, 0)),
                  pl.BlockSpec((None, cg, cg), lambda ci, gi: (ci, 0, 0)),
                  pl.BlockSpec((cg, d), lambda ci, gi: (gi, 0))],
        out_specs=pl.BlockSpec((None, cg, d), lambda ci, gi: (ci, gi, 0)),
        compiler_params=_cparams("parallel", "parallel"),
        name="fold_channel_dft",
    )(cs_hi, cs_lo, w_o.astype(F32))
    return out[0], out[1]


def _fft2_ln_router(a1, gbig, mr, mi, b_o, x3, ln_g, ln_b, router):
    b, s, d = x3.shape
    s1 = FFT_S1
    s2 = s // s1
    kb = FFT_KB
    nblk = s2 // kb
    rows = s1 * kb
    t = b * s
    a6 = a1.reshape(b, 2, nblk, kb, s1, d)
    x5 = x3.reshape(b, s1, nblk, kb, d)
    tail_in, outs = _router_specs(d, rows)
    m = 2 * rows
    return pl.pallas_call(
        _fft2_ln_kernel,
        out_shape=_router_out_shapes(t, d),
        grid=(nblk * b,),
        in_specs=[pl.BlockSpec((None, 2, None, kb, s1, d), lambda i: (i % b, 0, i // b, 0, 0, 0)),
                  pl.BlockSpec((None, m, m), lambda i: (i // b, 0, 0)),
                  pl.BlockSpec((d, d), lambda i: (0, 0)),
                  pl.BlockSpec((d, d), lambda i: (0, 0)),
                  pl.BlockSpec((1, d), lambda i: (0, 0)),
                  pl.BlockSpec((None, s1, None, kb, d), lambda i: (i % b, 0, i // b, 0, 0))] + tail_in,
        out_specs=outs,
        scratch_shapes=[pltpu.VMEM((N_EXPERTS, 1), F32)],
        compiler_params=_cparams("arbitrary"),
        name="fft2_ln_router",
    )(a6, gbig, mr, mi, b_o, x5, ln_g, ln_b, *router)


def _dispatch_kernel(dest_ref, x_ref, xs_in_ref, xs_ref, sem):
    del xs_in_ref
    rows = x_ref.shape[0]

    def body(r, carry):
        for kk in range(2):
            dst = dest_ref[0, 0, kk * rows + r]
            pltpu.make_async_copy(x_ref.at[pl.ds(r, 1), :], xs_ref.at[pl.ds(dst, 1), :], sem).start()
        return carry

    lax.fori_loop(0, rows, body, 0, unroll=8)
    for _ in range(2):
        pltpu.make_async_copy(x_ref, xs_ref.at[pl.ds(0, rows), :], sem).wait()


def _dispatch(xn, dest_blocks, p_rows):
    t, d = xn.shape
    rows = DMA_TILE
    xs0 = jnp.zeros((p_rows, d), F32)
    return pl.pallas_call(
        _dispatch_kernel,
        out_shape=jax.ShapeDtypeStruct((p_rows, d), F32),
        grid=(t // rows,),
        in_specs=[pl.BlockSpec((1, 1, 2 * rows), lambda i: (i, 0, 0), memory_space=pltpu.SMEM),
                  pl.BlockSpec((rows, d), lambda i: (i, 0)),
                  pl.BlockSpec(memory_space=pl.ANY)],
        out_specs=pl.BlockSpec(memory_space=pl.ANY),
        scratch_shapes=[pltpu.SemaphoreType.DMA(())],
        input_output_aliases={2: 0},
        compiler_params=_cparams("arbitrary"),
        name="moe_dispatch",
    )(dest_blocks, xn, xs0)


def _expert_kernel(bexp_ref, nused_ref, x_ref, w13_ref, w2_ref, o_ref, *, de):
    del bexp_ref

    @pl.when(pl.program_id(0) < nused_ref[0])
    def _():
        xb = x_ref[...].astype(BF16)
        h = jnp.dot(xb, w13_ref[...], preferred_element_type=F32)
        a = h[:, :de]
        act = (a / (1.0 + jnp.exp(-a))) * h[:, de:]
        o_ref[...] = jnp.dot(act.astype(BF16), w2_ref[...], preferred_element_type=F32)

    @pl.when(pl.program_id(0) >= nused_ref[0])
    def _():
        o_ref[...] = jnp.zeros(o_ref.shape, F32)


def _expert_mlp(xs, w13, w2, block_exp, n_used):
    p_rows, d = xs.shape
    de = w2.shape[1]
    nb = p_rows // EXPERT_BLOCK

    def in_row_map(i, bexp, nused):
        return (jnp.minimum(i, nused[0] - 1), 0)

    grid_spec = pltpu.PrefetchScalarGridSpec(
        num_scalar_prefetch=2,
        grid=(nb,),
        in_specs=[pl.BlockSpec((EXPERT_BLOCK, d), in_row_map),
                  pl.BlockSpec((None, d, 2 * de), lambda i, bexp, nused: (bexp[i], 0, 0)),
                  pl.BlockSpec((None, de, d), lambda i, bexp, nused: (bexp[i], 0, 0))],
        out_specs=pl.BlockSpec((EXPERT_BLOCK, d), lambda i, bexp, nused: (i, 0)),
    )
    return pl.pallas_call(
        functools.partial(_expert_kernel, de=de),
        out_shape=jax.ShapeDtypeStruct((p_rows, d), F32),
        grid_spec=grid_spec,
        compiler_params=_cparams("arbitrary"),
        name="expert_mlp",
    )(block_exp, n_used, xs, w13, w2)


def _combine_ln_kernel(dest_ref, ys_ref, x_ref, gates_ref, g_ref, b_ref, o_ref, buf_ref, sem):
    rows = x_ref.shape[0]

    def body(r, carry):
        for kk in range(2):
            src = dest_ref[0, 0, kk * rows + r]
            pltpu.make_async_copy(ys_ref.at[pl.ds(src, 1), :], buf_ref.at[kk, pl.ds(r, 1), :], sem).start()
        return carry

    lax.fori_loop(0, rows, body, 0, unroll=8)
    for kk in range(2):
        pltpu.make_async_copy(ys_ref.at[pl.ds(0, rows), :], buf_ref.at[kk], sem).wait()

    gates = gates_ref[...]
    m = gates[:, 0:1] * buf_ref[0] + gates[:, 1:2] * buf_ref[1]
    y = _ln(ALPHA * x_ref[...] + m, g_ref[...], b_ref[...])
    o_ref[...] = y.reshape(o_ref.shape)


def _combine_ln(ys, xn, gates, dest_blocks, ln_g, ln_b, out_shape, out_spec):
    t, d = xn.shape
    rows = DMA_TILE
    return pl.pallas_call(
        _combine_ln_kernel,
        out_shape=out_shape,
        grid=(t // rows,),
        in_specs=[pl.BlockSpec((1, 1, 2 * rows), lambda i: (i, 0, 0), memory_space=pltpu.SMEM),
                  pl.BlockSpec(memory_space=pl.ANY),
                  pl.BlockSpec((rows, d), lambda i: (i, 0)),
                  pl.BlockSpec((rows, 2), lambda i: (i, 0)),
                  pl.BlockSpec((1, d), lambda i: (0, 0)),
                  pl.BlockSpec((1, d), lambda i: (0, 0))],
        out_specs=out_spec,
        scratch_shapes=[pltpu.VMEM((2, rows, d), F32), pltpu.SemaphoreType.DMA(())],
        compiler_params=_cparams("arbitrary"),
        name="moe_combine_ln",
    )(dest_blocks, ys, xn, gates, ln_g, ln_b)


def _moe_ln(xn, route, counts, w13, w2, ln_g, ln_b, out_shape, out_spec):
    t, d = xn.shape
    blk = EXPERT_BLOCK
    nb = (2 * t) // blk + N_EXPERTS
    eid = route[0:2].astype(jnp.int32)
    rank = route[4:6].astype(jnp.int32)
    cnt = counts[:, 0].astype(jnp.int32)
    padded = (cnt + blk - 1) // blk * blk
    pends = jnp.cumsum(padded)
    pstarts = pends - padded
    dest = pstarts[eid] + rank
    nsteps = t // DMA_TILE
    dest_blocks = jnp.transpose(dest.reshape(2, nsteps, DMA_TILE), (1, 0, 2)).reshape(nsteps, 1, 2 * DMA_TILE)
    block_exp = jnp.minimum(
        jnp.searchsorted(pends, jnp.arange(nb, dtype=jnp.int32) * blk, side='right'),
        N_EXPERTS - 1).astype(jnp.int32)
    n_used = (pends[-1:] // blk).astype(jnp.int32)
    gates = jnp.transpose(route[2:4])

    xs = _dispatch(xn, dest_blocks, nb * blk)
    ys = _expert_mlp(xs, w13, w2, block_exp, n_used)
    return _combine_ln(ys, xn, gates, dest_blocks, ln_g, ln_b, out_shape, out_spec)


def _router_params(w_group, b_group, w_fine, b_fine, rows):
    d = w_group.shape[0]
    w = jnp.zeros((ROUTER_ROWS, d), F32)
    w = w.at[0:N_GROUPS].set(jnp.transpose(w_group).astype(F32))
    w = w.at[FINE_ROW0:FINE_ROW0 + N_EXPERTS].set(jnp.transpose(w_fine).astype(F32))
    bias = jnp.zeros((ROUTER_ROWS, 1), F32).at[N_GROUPS:FINE_ROW0, 0].set(NEG_BIG)
    bias = bias.at[0:N_GROUPS, 0].set(b_group.astype(F32))
    bias = bias.at[FINE_ROW0:FINE_ROW0 + N_EXPERTS, 0].set(b_fine.astype(F32))
    w_hi, w_lo = _split_bf16(w)
    tri = (jnp.arange(rows)[:, None] <= jnp.arange(rows)[None, :]).astype(BF16)
    return w_hi, w_lo, bias, tri


def _row(v):
    return v.reshape(1, -1).astype(F32)


def _trunk(x, p, shared):
    b, s, d = x.shape
    t = b * s
    h = d // HEAD_W
    assert s % ATT_TILE == 0 and t % ROW_TILE == 0 and s % (FFT_S1 * FFT_KB) == 0
    assert ROW_TILE == FFT_S1 * FFT_KB and ROW_TILE % DMA_TILE == 0
    flat_shape = jax.ShapeDtypeStruct((t, d), F32)
    flat_spec = pl.BlockSpec((DMA_TILE, d), lambda i: (i, 0))
    x2 = x.reshape(t, d)
    for i in range(DEPTH):
        j = i // 2
        if i % 2 == 0:
            qkv = _qkv_proj(x2, shared["wqkv"][j])
            att = _attention(qkv, shared["bias"], shared["lam"][j], shared["subln_g"][j], b, s, d,
                             shared["lam_init"][j])
            xn, route, counts = _proj_ln_router(att, shared["wo"][j], x2, _row(p["ln1_g"][i]),
                                                _row(p["ln1_b"][i]), shared["router"][i])
            x2 = _moe_ln(xn, route, counts, shared["w13"][i], shared["w2"][i],
                         _row(p["ln2_g"][i]), _row(p["ln2_b"][i]), flat_shape, flat_spec)
        else:
            x3 = x2.reshape(b, s, d)
            a1 = _fft_stage1(x3)
            mr, mi = shared["fold"][(j, s)]
            xn, route, counts = _fft2_ln_router(a1, shared["gbig"][s], mr, mi, _row(p["fnet_b_o"][j]), x3,
                                                _row(p["ln1_g"][i]), _row(p["ln1_b"][i]), shared["router"][i])
            nblk = s // (FFT_S1 * FFT_KB)
            halves = ROW_TILE // DMA_TILE
            k1_rows = FFT_S1 // halves
            out_shape = jax.ShapeDtypeStruct((b, FFT_S1, nblk, FFT_KB, d), F32)
            out_spec = pl.BlockSpec(
                (None, k1_rows, None, FFT_KB, d),
                lambda i2: ((i2 // halves) % b, i2 % halves, (i2 // halves) // b, 0, 0))
            y5 = _moe_ln(xn, route, counts, shared["w13"][i], shared["w2"][i],
                         _row(p["ln2_g"][i]), _row(p["ln2_b"][i]), out_shape, out_spec)
            x2 = y5.reshape(t, d)
    return x2.reshape(b, s, d)


def kernel(x_prompt, x_sample, rel_bias, attn_w_qkv, attn_lambda_q1, attn_lambda_k1, attn_lambda_q2, attn_lambda_k2, attn_subln_g, attn_w_o, fnet_w_o, fnet_b_o, ln1_g, ln1_b, ln2_g, ln2_b, moe_w_group, moe_b_group, moe_w_fine, moe_b_fine, moe_w1, moe_w3, moe_w2):
    d = x_prompt.shape[-1]
    p = dict(fnet_b_o=fnet_b_o, ln1_g=ln1_g, ln1_b=ln1_b, ln2_g=ln2_g, ln2_b=ln2_b)
    n_attn = attn_w_qkv.shape[0]
    n_fnet = fnet_w_o.shape[0]
    seqs = sorted({x_prompt.shape[1], x_sample.shape[1]})

    qscale = jnp.concatenate([jnp.full((d,), HEAD_DIM ** -0.5 * LOG2E, F32), jnp.ones((2 * d,), F32)])
    shared = dict(
        wqkv=[(attn_w_qkv[j].astype(F32) * qscale).astype(BF16) for j in range(n_attn)],
        wo=[attn_w_o[j].astype(BF16) for j in range(n_attn)],
        subln_g=[_row(attn_subln_g[j]) for j in range(n_attn)],
        lam_init=[0.8 - 0.6 * math.exp(-0.3 * (2 * j)) for j in range(n_attn)],
        bias=_bias_tiles(rel_bias, ATT_TILE),
        router=[_router_params(moe_w_group[i], moe_b_group[i], moe_w_fine[i], moe_b_fine[i], ROW_TILE)
                for i in range(DEPTH)],
        w13=[jnp.concatenate([moe_w1[i], moe_w3[i]], axis=-1).astype(BF16) for i in range(DEPTH)],
        w2=[moe_w2[i].astype(BF16) for i in range(DEPTH)],
        gbig={s: _fft_stage2_matrix(s) for s in seqs},
        fold={(j, s): _fold_channel_dft(fnet_w_o[j], s) for j in range(n_fnet) for s in seqs},
    )
    shared["lam"] = [
        (jnp.exp(jnp.sum(attn_lambda_q1[j].astype(F32) * attn_lambda_k1[j].astype(F32)))
         - jnp.exp(jnp.sum(attn_lambda_q2[j].astype(F32) * attn_lambda_k2[j].astype(F32)))
         + shared["lam_init"][j]).reshape(1).astype(F32)
        for j in range(n_attn)]

    return (_trunk(x_prompt, p, shared), _trunk(x_sample, p, shared))
```

```python
import functools
import math

import numpy as np
import jax
import jax.numpy as jnp
from jax import lax
from jax.experimental import pallas as pl
from jax.experimental.pallas import tpu as pltpu

F32 = jnp.float32
BF16 = jnp.bfloat16

HEAD_DIM = 64
HEAD_W = 2 * HEAD_DIM
REL_BUCKETS = 32
REL_MAX_DIST = 128
FNET_GROUPS = 4
N_GROUPS = 4
EXPERTS_PER_GROUP = 8
N_EXPERTS = N_GROUPS * EXPERTS_PER_GROUP
DEPTH = 2
ALPHA = (2 * DEPTH) ** 0.25
LN_EPS = 1e-5
LOG2E = 1.4426950408889634

LANES = 128
SUBLANES = 8
VMEM_LIMIT_BYTES = 52 * 1024 * 1024

ROW_TILE = 512
ATT_TILE = 512
BIAS_TILES = 5
DMA_TILE = 256
EXPERT_BLOCK = 256
FFT_S1 = 64
FFT_KB = SUBLANES
ROUTER_ROWS = 48
FINE_ROW0 = 8
NEG_BIG = -1e30


def _cparams(*sem):
    return pltpu.CompilerParams(dimension_semantics=sem, vmem_limit_bytes=VMEM_LIMIT_BYTES)


def _split_bf16(a):
    hi = a.astype(BF16)
    lo = (a - hi.astype(F32)).astype(BF16)
    return hi, lo


def _qkv_kernel(x_ref, w_ref, o_ref, *, d):
    xb = x_ref[...].astype(BF16)
    for c in range(w_ref.shape[1] // d):
        o_ref[:, c * d:(c + 1) * d] = jnp.dot(
            xb, w_ref[:, c * d:(c + 1) * d], preferred_element_type=F32).astype(BF16)


def _qkv_proj(x2, w_bf16):
    t, d = x2.shape
    n = w_bf16.shape[1]
    return pl.pallas_call(
        functools.partial(_qkv_kernel, d=d),
        out_shape=jax.ShapeDtypeStruct((t, n), BF16),
        grid=(t // ROW_TILE,),
        in_specs=[pl.BlockSpec((ROW_TILE, d), lambda i: (i, 0)),
                  pl.BlockSpec((d, n), lambda i: (0, 0))],
        out_specs=pl.BlockSpec((ROW_TILE, n), lambda i: (i, 0)),
        compiler_params=_cparams("parallel"),
        name="qkv_proj",
    )(x2, w_bf16)


def _attn_kernel(lam_ref, q_ref, k_ref, v_ref, bias_ref, g_ref, o_ref,
                 qbd_ref, s_ref, m_ref, l_ref, acc_ref, *, tile, nk, out_scale):
    qi = pl.program_id(2)
    qf = q_ref[...].astype(F32)
    lane = lax.broadcasted_iota(jnp.int32, qf.shape, 1)
    qbd_ref[0:tile, :] = jnp.where(lane < HEAD_DIM, qf, 0.0).astype(BF16)
    qbd_ref[tile:2 * tile, :] = jnp.where(lane >= HEAD_DIM, qf, 0.0).astype(BF16)
    m_ref[...] = jnp.full(m_ref.shape, NEG_BIG, F32)
    l_ref[...] = jnp.zeros(l_ref.shape, F32)
    acc_ref[...] = jnp.zeros(acc_ref.shape, F32)
    nt = (((1,), (1,)), ((), ()))
    tn = (((0,), (0,)), ((), ()))

    def scores(kb, slot):
        off = pl.multiple_of(kb * tile, tile)
        b = bias_ref[jnp.clip(kb - qi, -2, 2) + 2]
        s = lax.dot_general(k_ref[pl.ds(off, tile), :], qbd_ref[...], nt, preferred_element_type=F32)
        s_ref[slot] = s + jnp.concatenate([b, b], axis=1)

    def softmax_pv(kb, slot):
        off = pl.multiple_of(kb * tile, tile)
        s = s_ref[slot]
        m_old = m_ref[...]
        m_new = jnp.maximum(m_old, jnp.max(s, axis=0, keepdims=True))
        alpha = jnp.exp2(m_old - m_new)
        p = jnp.exp2(s - m_new)
        l_ref[...] = alpha * l_ref[...] + jnp.sum(p, axis=0, keepdims=True)
        pv = lax.dot_general(v_ref[pl.ds(off, tile), :], p.astype(BF16), tn, preferred_element_type=F32)
        acc_ref[...] = acc_ref[...] * alpha + pv
        m_ref[...] = m_new

    scores(0, 0)

    def pair(j, carry):
        kb = 2 * j
        scores(kb + 1, 1)
        softmax_pv(kb, 0)
        scores(kb + 2, 0)
        softmax_pv(kb + 1, 1)
        return carry

    lax.fori_loop(0, nk // 2 - 1, pair, 0)
    scores(nk - 1, 1)
    softmax_pv(nk - 2, 0)
    softmax_pv(nk - 1, 1)

    lam = lam_ref[0]
    inv = 1.0 / l_ref[...]
    acc = acc_ref[...]
    o_t = acc[:, :tile] * inv[:, :tile] - lam * (acc[:, tile:] * inv[:, tile:])
    ms = jnp.mean(o_t * o_t, axis=0, keepdims=True)
    o_t = o_t * lax.rsqrt(ms + LN_EPS)
    o_ref[...] = ((o_t.T * g_ref[...]) * out_scale).astype(BF16)


def _attention(qkv, bias_tab, lam, subln_g, b, s, d, lam_init):
    h = d // HEAD_W
    tile = ATT_TILE
    nq = s // tile
    assert nq % 2 == 0 and nq >= 2
    kern = functools.partial(_attn_kernel, tile=tile, nk=nq, out_scale=1.0 - lam_init)
    return pl.pallas_call(
        kern,
        out_shape=jax.ShapeDtypeStruct((b * s, d), BF16),
        grid=(b, h, nq),
        in_specs=[
            pl.BlockSpec(memory_space=pltpu.SMEM),
            pl.BlockSpec((tile, HEAD_W), lambda bi, hi, qi: (bi * nq + qi, hi)),
            pl.BlockSpec((s, HEAD_W), lambda bi, hi, qi: (bi, h + hi)),
            pl.BlockSpec((s, HEAD_W), lambda bi, hi, qi: (bi, 2 * h + hi)),
            pl.BlockSpec((None, BIAS_TILES, tile, tile), lambda bi, hi, qi: (hi, 0, 0, 0)),
            pl.BlockSpec((1, HEAD_W), lambda bi, hi, qi: (0, 0)),
        ],
        out_specs=pl.BlockSpec((tile, HEAD_W), lambda bi, hi, qi: (bi * nq + qi, hi)),
        scratch_shapes=[pltpu.VMEM((2 * tile, HEAD_W), BF16),
                        pltpu.VMEM((2, tile, 2 * tile), F32),
                        pltpu.VMEM((1, 2 * tile), F32),
                        pltpu.VMEM((1, 2 * tile), F32),
                        pltpu.VMEM((HEAD_W, 2 * tile), F32)],
        compiler_params=_cparams("parallel", "parallel", "arbitrary"),
        name="diff_attention",
    )(lam, qkv, qkv, qkv, bias_tab, subln_g)


def _rel_bucket(rel):
    nb = REL_BUCKETS // 2
    max_exact = nb // 2
    ret = jnp.where(rel > 0, nb, 0)
    n = jnp.abs(rel)
    nf = jnp.maximum(n, 1).astype(F32)
    large = max_exact + (jnp.log(nf / max_exact) / math.log(REL_MAX_DIST / max_exact)
                         * (nb - max_exact)).astype(jnp.int32)
    large = jnp.minimum(large, nb - 1)
    return ret + jnp.where(n < max_exact, n, large)


def _bias_kernel(rb_ref, bucket_ref, o_ref, *, tile):
    h = pl.program_id(0)
    half = REL_BUCKETS // 2
    ranges = ((half - 1, half), (0, half), (0, REL_BUCKETS), (half, REL_BUCKETS), (REL_BUCKETS - 1, REL_BUCKETS))
    rows = 4 * SUBLANES
    for t in range(BIAS_TILES):
        def body(i, carry, t=t):
            r0 = pl.multiple_of(i * rows, rows)
            bk = bucket_ref[t, pl.ds(r0, rows), :]
            acc = jnp.zeros(bk.shape, F32)
            for bkt in range(*ranges[t]):
                acc = jnp.where(bk == bkt, rb_ref[h, bkt], acc)
            o_ref[t, pl.ds(r0, rows), :] = acc
            return carry
        lax.fori_loop(0, tile // rows, body, 0)


def _bias_tables(rel_bias, tile):
    assert tile >= REL_MAX_DIST
    nh = rel_bias.shape[1]
    dd = jnp.arange(-2, 3, dtype=jnp.int32)[:, None, None]
    kk = jnp.arange(tile, dtype=jnp.int32)[None, :, None]
    qq = jnp.arange(tile, dtype=jnp.int32)[None, None, :]
    bucket = _rel_bucket(dd * tile + kk - qq)
    rb = jnp.transpose(rel_bias.astype(F32)) * LOG2E
    return pl.pallas_call(
        functools.partial(_bias_kernel, tile=tile),
        out_shape=jax.ShapeDtypeStruct((nh, BIAS_TILES, tile, tile), F32),
        grid=(nh,),
        in_specs=[pl.BlockSpec(memory_space=pltpu.SMEM),
                  pl.BlockSpec((BIAS_TILES, tile, tile), lambda i: (0, 0, 0))],
        out_specs=pl.BlockSpec((None, BIAS_TILES, tile, tile), lambda i: (i, 0, 0, 0)),
        compiler_params=_cparams("parallel"),
        name="rel_bias_tiles",
    )(rb, bucket)


def _ln(z, g, b):
    mu = jnp.mean(z, axis=-1, keepdims=True)
    zc = z - mu
    var = jnp.mean(zc * zc, axis=-1, keepdims=True)
    return zc * lax.rsqrt(var + LN_EPS) * g + b


def _route_epilogue(xn, wr_hi_ref, wr_lo_ref, br_ref, tri_ref, route_ref, counts_ref, base_ref):
    rows = xn.shape[0]
    x_hi, x_lo = _split_bf16(xn)
    nt = (((1,), (1,)), ((), ()))
    logits = (lax.dot_general(wr_hi_ref[...], x_hi, nt, preferred_element_type=F32)
              + lax.dot_general(wr_hi_ref[...], x_lo, nt, preferred_element_type=F32)
              + lax.dot_general(wr_lo_ref[...], x_hi, nt, preferred_element_type=F32))
    logits = logits + br_ref[...]

    lg = logits[0:FINE_ROW0]
    gmax = jnp.max(lg, axis=0, keepdims=True)
    pg_top = 1.0 / jnp.sum(jnp.exp(lg - gmax), axis=0, keepdims=True)
    gio = lax.broadcasted_iota(jnp.int32, lg.shape, 0).astype(F32)
    g_idx = jnp.min(jnp.where(lg == gmax, gio, float(FINE_ROW0)), axis=0, keepdims=True)

    sel = jnp.zeros((EXPERTS_PER_GROUP, rows), F32)
    for g in range(N_GROUPS):
        r0 = FINE_ROW0 + g * EXPERTS_PER_GROUP
        sel = jnp.where(g_idx == float(g), logits[r0:r0 + EXPERTS_PER_GROUP], sel)
    eio = lax.broadcasted_iota(jnp.int32, sel.shape, 0).astype(F32)
    v1 = jnp.max(sel, axis=0, keepdims=True)
    i1 = jnp.min(jnp.where(sel == v1, eio, float(EXPERTS_PER_GROUP)), axis=0, keepdims=True)
    rest = jnp.where(eio == i1, -jnp.inf, sel)
    v2 = jnp.max(rest, axis=0, keepdims=True)
    i2 = jnp.min(jnp.where(rest == v2, eio, float(EXPERTS_PER_GROUP)), axis=0, keepdims=True)
    e2 = jnp.exp(v2 - v1)
    den = 1.0 + e2
    gate1 = pg_top / den
    gate2 = pg_top * e2 / den
    eid1 = g_idx * float(EXPERTS_PER_GROUP) + i1
    eid2 = g_idx * float(EXPERTS_PER_GROUP) + i2

    xio = lax.broadcasted_iota(jnp.int32, (N_EXPERTS, rows), 0).astype(F32)
    oh1 = xio == eid1
    oh2 = xio == eid2
    cnt = jnp.where(oh1 | oh2, 1.0, 0.0)
    incl = jnp.dot(cnt.astype(BF16), tri_ref[...], preferred_element_type=F32)
    before = incl - cnt + base_ref[...]
    rank1 = jnp.sum(jnp.where(oh1, before, 0.0), axis=0, keepdims=True)
    rank2 = jnp.sum(jnp.where(oh2, before, 0.0), axis=0, keepdims=True)
    base_new = base_ref[...] + jnp.sum(cnt, axis=1, keepdims=True)
    base_ref[...] = base_new
    counts_ref[...] = jnp.broadcast_to(base_new, counts_ref.shape)
    rio = lax.broadcasted_iota(jnp.int32, route_ref.shape, 0)
    out = jnp.zeros(route_ref.shape, F32)
    for r, val in enumerate((eid1, eid2, gate1, gate2, rank1, rank2)):
        out = jnp.where(rio == r, val, out)
    route_ref[...] = out


def _proj_ln_kernel(a_ref, w_ref, x_ref, g_ref, b_ref, wr_hi_ref, wr_lo_ref, br_ref, tri_ref,
                    xn_ref, route_ref, counts_ref, base_ref):
    @pl.when(pl.program_id(0) == 0)
    def _():
        base_ref[...] = jnp.zeros(base_ref.shape, F32)

    h = jnp.dot(a_ref[...], w_ref[...], preferred_element_type=F32)
    xn = _ln(ALPHA * x_ref[...] + h, g_ref[...], b_ref[...])
    xn_ref[...] = xn
    _route_epilogue(xn, wr_hi_ref, wr_lo_ref, br_ref, tri_ref, route_ref, counts_ref, base_ref)


def _fft2_ln_kernel(a_ref, gbig_ref, mr_ref, mi_ref, bo_ref, x_ref, g_ref, b_ref,
                    wr_hi_ref, wr_lo_ref, br_ref, tri_ref,
                    xn_ref, route_ref, counts_ref, base_ref):
    @pl.when(pl.program_id(0) == 0)
    def _():
        base_ref[...] = jnp.zeros(base_ref.shape, F32)

    rows = xn_ref.shape[0]
    d = xn_ref.shape[1]
    a = a_ref[...].reshape(2 * rows, d)
    v = jnp.dot(gbig_ref[...], a, preferred_element_type=F32)
    h = (jnp.dot(v[:rows].astype(BF16), mr_ref[...], preferred_element_type=F32)
         + jnp.dot(v[rows:].astype(BF16), mi_ref[...], preferred_element_type=F32)
         + bo_ref[...])
    xn = _ln(ALPHA * x_ref[...].reshape(rows, d) + h, g_ref[...], b_ref[...])
    xn_ref[...] = xn
    _route_epilogue(xn, wr_hi_ref, wr_lo_ref, br_ref, tri_ref, route_ref, counts_ref, base_ref)


def _router_specs(d, rows):
    const2 = lambda i: (0, 0)
    ins = [pl.BlockSpec((1, d), const2), pl.BlockSpec((1, d), const2),
           pl.BlockSpec((ROUTER_ROWS, d), const2), pl.BlockSpec((ROUTER_ROWS, d), const2),
           pl.BlockSpec((ROUTER_ROWS, 1), const2), pl.BlockSpec((rows, rows), const2)]
    outs = [pl.BlockSpec((rows, d), lambda i: (i, 0)),
            pl.BlockSpec((SUBLANES, rows), lambda i: (0, i)),
            pl.BlockSpec((N_EXPERTS, LANES), const2)]
    return ins, outs


def _router_out_shapes(t, d):
    return (jax.ShapeDtypeStruct((t, d), F32),
            jax.ShapeDtypeStruct((SUBLANES, t), F32),
            jax.ShapeDtypeStruct((N_EXPERTS, LANES), F32))


def _proj_ln_router(att, wo_bf16, x2, ln_g, ln_b, router):
    t, d = x2.shape
    rows = ROW_TILE
    tail_in, outs = _router_specs(d, rows)
    return pl.pallas_call(
        _proj_ln_kernel,
        out_shape=_router_out_shapes(t, d),
        grid=(t // rows,),
        in_specs=[pl.BlockSpec((rows, d), lambda i: (i, 0)),
                  pl.BlockSpec((d, d), lambda i: (0, 0)),
                  pl.BlockSpec((rows, d), lambda i: (i, 0))] + tail_in,
        out_specs=outs,
        scratch_shapes=[pltpu.VMEM((N_EXPERTS, 1), F32)],
        compiler_params=_cparams("arbitrary"),
        name="proj_ln_router",
    )(att, wo_bf16, x2, ln_g, ln_b, *router)


def _fft1_kernel(f_ref, x_ref, o_ref):
    for j in range(x_ref.shape[1]):
        xj = x_ref[:, j, :].astype(BF16)
        o_ref[j] = jnp.dot(f_ref[...], xj, preferred_element_type=F32).astype(BF16)


def _fft_stage1(x3):
    b, s, d = x3.shape
    s1 = FFT_S1
    s2 = s // s1
    kb = FFT_KB
    k = np.arange(s2)
    ang = 2.0 * np.pi * ((k[:, None] * k[None, :]) % s2) / s2
    f = np.stack([np.cos(ang), -np.sin(ang)])
    f = f.reshape(2, s2 // kb, kb, s2).transpose(1, 0, 2, 3).reshape(2 * s2, s2)
    f = jnp.asarray(f, dtype=BF16)
    n1_tile = SUBLANES
    return pl.pallas_call(
        _fft1_kernel,
        out_shape=jax.ShapeDtypeStruct((b, s1, 2 * s2, d), BF16),
        grid=(b, s1 // n1_tile),
        in_specs=[pl.BlockSpec((2 * s2, s2), lambda bi, ni: (0, 0)),
                  pl.BlockSpec((None, s2, n1_tile, d), lambda bi, ni: (bi, 0, ni, 0))],
        out_specs=pl.BlockSpec((None, n1_tile, 2 * s2, d), lambda bi, ni: (bi, ni, 0, 0)),
        compiler_params=_cparams("parallel", "parallel"),
        name="fft_stage1",
    )(f, x3.reshape(b, s2, s1, d))


def _fft_stage2_matrix(s):
    s1 = FFT_S1
    s2 = s // s1
    kb = FFT_KB
    k2 = np.arange(s2)[:, None, None]
    k1 = np.arange(s1)[None, :, None]
    n1 = np.arange(s1)[None, None, :]
    ang = 2.0 * np.pi * ((k1 * n1 * s2 + k2 * n1) % s) / s
    gr = jnp.asarray(np.cos(ang), dtype=BF16).reshape(s2 // kb, kb, s1, s1)
    gi = jnp.asarray(-np.sin(ang), dtype=BF16).reshape(s2 // kb, kb, s1, s1)
    eye = jnp.eye(kb, dtype=BF16)

    def expand(g):
        return jnp.transpose(g, (0, 2, 1, 3))[:, :, :, :, None] * eye[None, None, :, None, :]

    er, ei = expand(gr), expand(gi)
    top = jnp.stack([er, -ei], axis=4)
    bot = jnp.stack([ei, er], axis=4)
    big = jnp.stack([top, bot], axis=1)
    m = 2 * kb * s1
    return big.reshape(s2 // kb, m, m)


def _fold_kernel(cs_hi_ref, cs_lo_ref, w_ref, o_ref):
    w_hi, w_lo = _split_bf16(w_ref[...])
    o_ref[...] = (jnp.dot(cs_hi_ref[...], w_hi, preferred_element_type=F32)
                  + jnp.dot(cs_hi_ref[...], w_lo, preferred_element_type=F32)
                  + jnp.dot(cs_lo_ref[...], w_hi, preferred_element_type=F32)).astype(BF16)


def _fold_channel_dft(w_o, s):
    d = w_o.shape[0]
    cg = d // FNET_GROUPS
    c = np.arange(cg)
    ang = 2.0 * np.pi * ((c[:, None] * c[None, :]) % cg) / cg
    scale = 1.0 / math.sqrt(s * cg)
    cs = jnp.asarray(np.stack([np.cos(ang), np.sin(ang)]) * scale, dtype=F32)
    cs_hi, cs_lo = _split_bf16(cs)
    out = pl.pallas_call(
        _fold_kernel,
        out_shape=jax.ShapeDtypeStruct((2, d, d), BF16),
        grid=(2, FNET_GROUPS),
        in_specs=[pl.BlockSpec((None, cg, cg), lambda ci, gi: (ci, 0, 0)),
                  pl.BlockSpec((None, cg, cg), lambda ci, gi: (ci, 0, 0)),
                  pl.BlockSpec((cg, d), lambda ci, gi: (gi, 0))],
        out_specs=pl.BlockSpec((None, cg, d), lambda ci, gi: (ci, gi, 0)),
        compiler_params=_cparams("parallel", "parallel"),
        name="fold_channel_dft",
    )(cs_hi, cs_lo, w_o.astype(F32))
    return out[0], out[1]


def _fft2_ln_router(a1, gbig, mr, mi, b_o, x3, ln_g, ln_b, router):
    b, s, d = x3.shape
    s1 = FFT_S1
    s2 = s // s1
    kb = FFT_KB
    nblk = s2 // kb
    rows = s1 * kb
    t = b * s
    a5 = a1.reshape(b, s1, nblk, 2 * kb, d)
    x5 = x3.reshape(b, s1, nblk, kb, d)
    tail_in, outs = _router_specs(d, rows)
    m = 2 * rows
    return pl.pallas_call(
        _fft2_ln_kernel,
        out_shape=_router_out_shapes(t, d),
        grid=(nblk * b,),
        in_specs=[pl.BlockSpec((None, s1, None, 2 * kb, d), lambda i: (i % b, 0, i // b, 0, 0)),
                  pl.BlockSpec((None, m, m), lambda i: (i // b, 0, 0)),
                  pl.BlockSpec((d, d), lambda i: (0, 0)),
                  pl.BlockSpec((d, d), lambda i: (0, 0)),
                  pl.BlockSpec((1, d), lambda i: (0, 0)),
                  pl.BlockSpec((None, s1, None, kb, d), lambda i: (i % b, 0, i // b, 0, 0))] + tail_in,
        out_specs=outs,
        scratch_shapes=[pltpu.VMEM((N_EXPERTS, 1), F32)],
        compiler_params=_cparams("arbitrary"),
        name="fft2_ln_router",
    )(a5, gbig, mr, mi, b_o, x5, ln_g, ln_b, *router)


def _dispatch_kernel(meta_ref, dest_ref, x_ref, xs_ref, zero_ref, sem, zsem, *, blk, nb):
    rows = x_ref.shape[0]

    @pl.when(pl.program_id(0) == 0)
    def _():
        zero_ref[...] = jnp.zeros(zero_ref.shape, F32)
        n_used = meta_ref[2 * N_EXPERTS]
        fills = [(meta_ref[N_EXPERTS + e] > 0, meta_ref[e] - blk) for e in range(N_EXPERTS)]
        fills += [(n_used + i < nb, (n_used + i) * blk) for i in range(N_EXPERTS)]

        def fill(row0):
            return pltpu.make_async_copy(zero_ref, xs_ref.at[pl.ds(pl.multiple_of(row0, blk), blk), :], zsem)

        for cond, row0 in fills:
            pl.when(cond)(lambda row0=row0: fill(row0).start())
        for cond, row0 in fills:
            pl.when(cond)(lambda row0=row0: fill(row0).wait())

    def body(r, carry):
        for kk in range(2):
            dst = dest_ref[0, 0, kk * rows + r]
            pltpu.make_async_copy(x_ref.at[pl.ds(r, 1), :], xs_ref.at[pl.ds(dst, 1), :], sem).start(priority=kk)
        return carry

    lax.fori_loop(0, rows, body, 0, unroll=8)
    for _ in range(2):
        pltpu.make_async_copy(x_ref, xs_ref.at[pl.ds(0, rows), :], sem).wait()


def _dispatch(xn, meta, dest_blocks, p_rows):
    t, d = xn.shape
    rows = DMA_TILE
    blk = EXPERT_BLOCK
    return pl.pallas_call(
        functools.partial(_dispatch_kernel, blk=blk, nb=p_rows // blk),
        out_shape=jax.ShapeDtypeStruct((p_rows, d), F32),
        grid=(t // rows,),
        in_specs=[pl.BlockSpec(memory_space=pltpu.SMEM),
                  pl.BlockSpec((1, 1, 2 * rows), lambda i: (i, 0, 0), memory_space=pltpu.SMEM),
                  pl.BlockSpec((rows, d), lambda i: (i, 0))],
        out_specs=pl.BlockSpec(memory_space=pl.ANY),
        scratch_shapes=[pltpu.VMEM((blk, d), F32), pltpu.SemaphoreType.DMA(()), pltpu.SemaphoreType.DMA(())],
        compiler_params=_cparams("arbitrary"),
        name="moe_dispatch",
    )(meta, dest_blocks, xn)


def _expert_kernel(bexp_ref, nused_ref, x_ref, w13_ref, w2_ref, o_ref, *, de):
    del bexp_ref

    @pl.when(pl.program_id(0) < nused_ref[0])
    def _():
        xb = x_ref[...].astype(BF16)
        h = jnp.dot(xb, w13_ref[...], preferred_element_type=F32)
        a = h[:, :de]
        act = (a / (1.0 + jnp.exp(-a))) * h[:, de:]
        o_ref[...] = jnp.dot(act.astype(BF16), w2_ref[...], preferred_element_type=F32)

    @pl.when(pl.program_id(0) >= nused_ref[0])
    def _():
        o_ref[...] = jnp.zeros(o_ref.shape, F32)


def _expert_mlp(xs, w13, w2, block_exp, n_used):
    p_rows, d = xs.shape
    de = w2.shape[1]
    nb = p_rows // EXPERT_BLOCK

    def in_row_map(i, bexp, nused):
        return (jnp.minimum(i, nused[0] - 1), 0)

    grid_spec = pltpu.PrefetchScalarGridSpec(
        num_scalar_prefetch=2,
        grid=(nb,),
        in_specs=[pl.BlockSpec((EXPERT_BLOCK, d), in_row_map),
                  pl.BlockSpec((None, d, 2 * de), lambda i, bexp, nused: (bexp[i], 0, 0)),
                  pl.BlockSpec((None, de, d), lambda i, bexp, nused: (bexp[i], 0, 0))],
        out_specs=pl.BlockSpec((EXPERT_BLOCK, d), lambda i, bexp, nused: (i, 0)),
    )
    return pl.pallas_call(
        functools.partial(_expert_kernel, de=de),
        out_shape=jax.ShapeDtypeStruct((p_rows, d), F32),
        grid_spec=grid_spec,
        compiler_params=_cparams("arbitrary"),
        name="expert_mlp",
    )(block_exp, n_used, xs, w13, w2)


def _combine_ln_kernel(dest_ref, ys_ref, x_ref, gates_ref, g_ref, b_ref, o_ref, buf_ref, sem):
    rows = x_ref.shape[0]

    def body(r, carry):
        for kk in range(2):
            src = dest_ref[0, 0, kk * rows + r]
            pltpu.make_async_copy(ys_ref.at[pl.ds(src, 1), :], buf_ref.at[kk, pl.ds(r, 1), :],
                                  sem).start(priority=kk)
        return carry

    lax.fori_loop(0, rows, body, 0, unroll=8)
    for kk in range(2):
        pltpu.make_async_copy(ys_ref.at[pl.ds(0, rows), :], buf_ref.at[kk], sem).wait()

    gates = gates_ref[...]
    m = gates[:, 0:1] * buf_ref[0] + gates[:, 1:2] * buf_ref[1]
    y = _ln(ALPHA * x_ref[...] + m, g_ref[...], b_ref[...])
    o_ref[...] = y.reshape(o_ref.shape)


def _combine_ln(ys, xn, gates, dest_blocks, ln_g, ln_b, out_shape, out_spec):
    t, d = xn.shape
    rows = DMA_TILE
    return pl.pallas_call(
        _combine_ln_kernel,
        out_shape=out_shape,
        grid=(t // rows,),
        in_specs=[pl.BlockSpec((1, 1, 2 * rows), lambda i: (i, 0, 0), memory_space=pltpu.SMEM),
                  pl.BlockSpec(memory_space=pl.ANY),
                  pl.BlockSpec((rows, d), lambda i: (i, 0)),
                  pl.BlockSpec((rows, 2), lambda i: (i, 0)),
                  pl.BlockSpec((1, d), lambda i: (0, 0)),
                  pl.BlockSpec((1, d), lambda i: (0, 0))],
        out_specs=out_spec,
        scratch_shapes=[pltpu.VMEM((2, rows, d), F32), pltpu.SemaphoreType.DMA(())],
        compiler_params=_cparams("arbitrary"),
        name="moe_combine_ln",
    )(dest_blocks, ys, xn, gates, ln_g, ln_b)


def _moe_ln(xn, route, counts, w13, w2, ln_g, ln_b, out_shape, out_spec):
    t, d = xn.shape
    blk = EXPERT_BLOCK
    nb = (2 * t) // blk + N_EXPERTS
    eid = route[0:2].astype(jnp.int32)
    rank = route[4:6].astype(jnp.int32)
    cnt = counts[:, 0].astype(jnp.int32)
    padded = (cnt + blk - 1) // blk * blk
    pends = jnp.cumsum(padded)
    pstarts = pends - padded
    experts = jnp.arange(N_EXPERTS, dtype=jnp.int32)
    dest = jnp.sum(jnp.where(eid[:, :, None] == experts, pstarts, 0), axis=-1) + rank
    nsteps = t // DMA_TILE
    dest_blocks = jnp.transpose(dest.reshape(2, nsteps, DMA_TILE), (1, 0, 2)).reshape(nsteps, 1, 2 * DMA_TILE)
    block_row0 = jnp.arange(nb, dtype=jnp.int32)[:, None] * blk
    block_exp = jnp.minimum(jnp.sum((pends[None, :] <= block_row0).astype(jnp.int32), axis=1), N_EXPERTS - 1)
    n_used = (pends[-1:] // blk).astype(jnp.int32)
    meta = jnp.concatenate([pends, padded, n_used]).astype(jnp.int32)
    gates = jnp.transpose(route[2:4])

    xs = _dispatch(xn, meta, dest_blocks, nb * blk)
    ys = _expert_mlp(xs, w13, w2, block_exp, n_used)
    return _combine_ln(ys, xn, gates, dest_blocks, ln_g, ln_b, out_shape, out_spec)


def _router_params(w_group, b_group, w_fine, b_fine, rows):
    d = w_group.shape[0]
    w = jnp.zeros((ROUTER_ROWS, d), F32)
    w = w.at[0:N_GROUPS].set(jnp.transpose(w_group).astype(F32))
    w = w.at[FINE_ROW0:FINE_ROW0 + N_EXPERTS].set(jnp.transpose(w_fine).astype(F32))
    bias = jnp.zeros((ROUTER_ROWS, 1), F32).at[N_GROUPS:FINE_ROW0, 0].set(NEG_BIG)
    bias = bias.at[0:N_GROUPS, 0].set(b_group.astype(F32))
    bias = bias.at[FINE_ROW0:FINE_ROW0 + N_EXPERTS, 0].set(b_fine.astype(F32))
    w_hi, w_lo = _split_bf16(w)
    tri = (jnp.arange(rows)[:, None] <= jnp.arange(rows)[None, :]).astype(BF16)
    return w_hi, w_lo, bias, tri


def _row(v):
    return v.reshape(1, -1).astype(F32)


def _trunk(x, p, shared):
    b, s, d = x.shape
    t = b * s
    h = d // HEAD_W
    assert s % ATT_TILE == 0 and t % ROW_TILE == 0 and s % (FFT_S1 * FFT_KB) == 0
    assert ROW_TILE == FFT_S1 * FFT_KB and ROW_TILE % DMA_TILE == 0
    flat_shape = jax.ShapeDtypeStruct((t, d), F32)
    flat_spec = pl.BlockSpec((DMA_TILE, d), lambda i: (i, 0))
    x2 = x.reshape(t, d)
    for i in range(DEPTH):
        j = i // 2
        if i % 2 == 0:
            qkv = _qkv_proj(x2, shared["wqkv"][j])
            att = _attention(qkv, shared["bias"], shared["lam"][j], shared["subln_g"][j], b, s, d,
                             shared["lam_init"][j])
            xn, route, counts = _proj_ln_router(att, shared["wo"][j], x2, _row(p["ln1_g"][i]),
                                                _row(p["ln1_b"][i]), shared["router"][i])
            x2 = _moe_ln(xn, route, counts, shared["w13"][i], shared["w2"][i],
                         _row(p["ln2_g"][i]), _row(p["ln2_b"][i]), flat_shape, flat_spec)
        else:
            x3 = x2.reshape(b, s, d)
            a1 = _fft_stage1(x3)
            mr, mi = shared["fold"][(j, s)]
            xn, route, counts = _fft2_ln_router(a1, shared["gbig"][s], mr, mi, _row(p["fnet_b_o"][j]), x3,
                                                _row(p["ln1_g"][i]), _row(p["ln1_b"][i]), shared["router"][i])
            nblk = s // (FFT_S1 * FFT_KB)
            halves = ROW_TILE // DMA_TILE
            k1_rows = FFT_S1 // halves
            out_shape = jax.ShapeDtypeStruct((b, FFT_S1, nblk, FFT_KB, d), F32)
            out_spec = pl.BlockSpec(
                (None, k1_rows, None, FFT_KB, d),
                lambda i2: ((i2 // halves) % b, i2 % halves, (i2 // halves) // b, 0, 0))
            y5 = _moe_ln(xn, route, counts, shared["w13"][i], shared["w2"][i],
                         _row(p["ln2_g"][i]), _row(p["ln2_b"][i]), out_shape, out_spec)
            x2 = y5.reshape(t, d)
    return x2.reshape(b, s, d)


def kernel(x_prompt, x_sample, rel_bias, attn_w_qkv, attn_lambda_q1, attn_lambda_k1, attn_lambda_q2, attn_lambda_k2, attn_subln_g, attn_w_o, fnet_w_o, fnet_b_o, ln1_g, ln1_b, ln2_g, ln2_b, moe_w_group, moe_b_group, moe_w_fine, moe_b_fine, moe_w1, moe_w3, moe_w2):
    d = x_prompt.shape[-1]
    p = dict(fnet_b_o=fnet_b_o, ln1_g=ln1_g, ln1_b=ln1_b, ln2_g=ln2_g, ln2_b=ln2_b)
    n_attn = attn_w_qkv.shape[0]
    n_fnet = fnet_w_o.shape[0]
    seqs = sorted({x_prompt.shape[1], x_sample.shape[1]})

    qscale = jnp.concatenate([jnp.full((d,), HEAD_DIM ** -0.5 * LOG2E, F32), jnp.ones((2 * d,), F32)])
    shared = dict(
        wqkv=[(attn_w_qkv[j].astype(F32) * qscale).astype(BF16) for j in range(n_attn)],
        wo=[attn_w_o[j].astype(BF16) for j in range(n_attn)],
        subln_g=[_row(attn_subln_g[j]) for j in range(n_attn)],
        lam_init=[0.8 - 0.6 * math.exp(-0.3 * (2 * j)) for j in range(n_attn)],
        bias=_bias_tables(rel_bias, ATT_TILE),
        router=[_router_params(moe_w_group[i], moe_b_group[i], moe_w_fine[i], moe_b_fine[i], ROW_TILE)
                for i in range(DEPTH)],
        w13=[jnp.concatenate([moe_w1[i], moe_w3[i]], axis=-1).astype(BF16) for i in range(DEPTH)],
        w2=[moe_w2[i].astype(BF16) for i in range(DEPTH)],
        gbig={s: _fft_stage2_matrix(s) for s in seqs},
        fold={(j, s): _fold_channel_dft(fnet_w_o[j], s) for j in range(n_fnet) for s in seqs},
    )
    shared["lam"] = [
        (jnp.exp(jnp.sum(attn_lambda_q1[j].astype(F32) * attn_lambda_k1[j].astype(F32)))
         - jnp.exp(jnp.sum(attn_lambda_q2[j].astype(F32) * attn_lambda_k2[j].astype(F32)))
         + shared["lam_init"][j]).reshape(1).astype(F32)
        for j in range(n_attn)]

    return (_trunk(x_prompt, p, shared), _trunk(x_sample, p, shared))
```

```python
import functools
import math

import numpy as np
import jax
import jax.numpy as jnp
from jax import lax
from jax.experimental import pallas as pl
from jax.experimental.pallas import tpu as pltpu

F32 = jnp.float32
BF16 = jnp.bfloat16

HEAD_DIM = 64
HEAD_W = 2 * HEAD_DIM
REL_BUCKETS = 32
REL_MAX_DIST = 128
FNET_GROUPS = 4
N_GROUPS = 4
EXPERTS_PER_GROUP = 8
N_EXPERTS = N_GROUPS * EXPERTS_PER_GROUP
DEPTH = 2
ALPHA = (2 * DEPTH) ** 0.25
LN_EPS = 1e-5
LOG2E = 1.4426950408889634

LANES = 128
SUBLANES = 8
VMEM_LIMIT_BYTES = 52 * 1024 * 1024

ROW_TILE = 512
ATT_TILE = 512
BIAS_TILES = 5
DMA_TILE = 256
EXPERT_BLOCK = 256
FFT_S1 = 64
FFT_KB = SUBLANES
ROUTER_ROWS = 48
FINE_ROW0 = 8
NEG_BIG = -1e30


def _cparams(*sem):
    return pltpu.CompilerParams(dimension_semantics=sem, vmem_limit_bytes=VMEM_LIMIT_BYTES)


def _split_bf16(a):
    hi = a.astype(BF16)
    lo = (a - hi.astype(F32)).astype(BF16)
    return hi, lo


def _qkv_kernel(x_ref, w_ref, o_ref, *, d):
    xb = x_ref[...].astype(BF16)
    for c in range(w_ref.shape[1] // d):
        o_ref[:, c * d:(c + 1) * d] = jnp.dot(
            xb, w_ref[:, c * d:(c + 1) * d], preferred_element_type=F32).astype(BF16)


def _qkv_proj(x2, w_bf16):
    t, d = x2.shape
    n = w_bf16.shape[1]
    return pl.pallas_call(
        functools.partial(_qkv_kernel, d=d),
        out_shape=jax.ShapeDtypeStruct((t, n), BF16),
        grid=(t // ROW_TILE,),
        in_specs=[pl.BlockSpec((ROW_TILE, d), lambda i: (i, 0)),
                  pl.BlockSpec((d, n), lambda i: (0, 0))],
        out_specs=pl.BlockSpec((ROW_TILE, n), lambda i: (i, 0)),
        compiler_params=_cparams("parallel"),
        name="qkv_proj",
    )(x2, w_bf16)


def _attn_kernel(lam_ref, far_ref, q_ref, k_ref, v_ref, bias_ref, g_ref, o_ref,
                 qbd_ref, s_ref, mx_ref, m_ref, l_ref, acc_ref, *, tile, nk, out_scale):
    hi = pl.program_id(1)
    qi = pl.program_id(2)
    qf = q_ref[...].astype(F32)
    lane = lax.broadcasted_iota(jnp.int32, qf.shape, 1)
    qbd_ref[0:tile, :] = jnp.where(lane < HEAD_DIM, qf, 0.0).astype(BF16)
    qbd_ref[tile:2 * tile, :] = jnp.where(lane >= HEAD_DIM, qf, 0.0).astype(BF16)
    m_ref[...] = jnp.full(m_ref.shape, NEG_BIG, F32)
    l_ref[...] = jnp.zeros(l_ref.shape, F32)
    acc_ref[...] = jnp.zeros(acc_ref.shape, F32)
    nt = (((1,), (1,)), ((), ()))
    tn = (((0,), (0,)), ((), ()))
    c_before = far_ref[hi, 0]
    c_after = far_ref[hi, 1]

    npairs = nk // 2 - 1
    near_lo = jnp.clip(qi // 2 - 1, 0, npairs)
    near_hi = jnp.clip(qi // 2 + 1, 0, npairs)

    def scores(kb, slot, with_bias):
        off = pl.multiple_of(kb * tile, tile)
        s = lax.dot_general(k_ref[pl.ds(off, tile), :], qbd_ref[...], nt, preferred_element_type=F32)
        if with_bias:
            b = bias_ref[jnp.clip(kb - qi, -2, 2) + 2]
            s = s + jnp.concatenate([b, b], axis=1)
        s_ref[slot] = s
        mx_ref[slot] = jnp.max(s, axis=0, keepdims=True)

    def softmax_pv(kb, slot):
        off = pl.multiple_of(kb * tile, tile)
        had_bias = (kb == 0) | (kb == nk - 1) | ((kb >= 2 * near_lo + 1) & (kb <= 2 * near_hi))
        shift = jnp.where(had_bias, 0.0, jnp.where(kb < qi, c_before, c_after))
        m_old = m_ref[...]
        m_new = jnp.maximum(m_old, mx_ref[slot] + shift)
        alpha = jnp.exp2(m_old - m_new)
        p = jnp.exp2(s_ref[slot] - (m_new - shift))
        l_ref[...] = alpha * l_ref[...] + jnp.sum(p, axis=0, keepdims=True)
        pv = lax.dot_general(v_ref[pl.ds(off, tile), :], p.astype(BF16), tn, preferred_element_type=F32)
        acc_ref[...] = acc_ref[...] * alpha + pv
        m_ref[...] = m_new

    def pairs(lo, hi_j, with_bias):
        def body(jj, carry):
            kb = 2 * jj
            scores(kb + 1, 1, with_bias)
            softmax_pv(kb, 0)
            scores(kb + 2, 0, with_bias)
            softmax_pv(kb + 1, 1)
            return carry
        lax.fori_loop(lo, hi_j, body, 0)

    scores(0, 0, True)
    pairs(0, near_lo, False)
    pairs(near_lo, near_hi, True)
    pairs(near_hi, npairs, False)
    scores(nk - 1, 1, True)
    softmax_pv(nk - 2, 0)
    softmax_pv(nk - 1, 1)

    lam = lam_ref[0]
    inv = 1.0 / l_ref[...]
    acc = acc_ref[...]
    o_t = acc[:, :tile] * inv[:, :tile] - lam * (acc[:, tile:] * inv[:, tile:])
    ms = jnp.mean(o_t * o_t, axis=0, keepdims=True)
    o_t = o_t * lax.rsqrt(ms + LN_EPS)
    o_ref[...] = ((o_t.T * g_ref[...]) * out_scale).astype(BF16)


def _attention(qkv, bias_tab, far, lam, subln_g, b, s, d, lam_init):
    h = d // HEAD_W
    tile = ATT_TILE
    nq = s // tile
    assert nq >= 2 and nq % 2 == 0
    kern = functools.partial(_attn_kernel, tile=tile, nk=nq, out_scale=1.0 - lam_init)
    return pl.pallas_call(
        kern,
        out_shape=jax.ShapeDtypeStruct((b * s, d), BF16),
        grid=(b, h, nq),
        in_specs=[
            pl.BlockSpec(memory_space=pltpu.SMEM),
            pl.BlockSpec(memory_space=pltpu.SMEM),
            pl.BlockSpec((tile, HEAD_W), lambda bi, hi, qi: (bi * nq + qi, hi)),
            pl.BlockSpec((s, HEAD_W), lambda bi, hi, qi: (bi, h + hi)),
            pl.BlockSpec((s, HEAD_W), lambda bi, hi, qi: (bi, 2 * h + hi)),
            pl.BlockSpec((None, BIAS_TILES, tile, tile), lambda bi, hi, qi: (hi, 0, 0, 0)),
            pl.BlockSpec((1, HEAD_W), lambda bi, hi, qi: (0, 0)),
        ],
        out_specs=pl.BlockSpec((tile, HEAD_W), lambda bi, hi, qi: (bi * nq + qi, hi)),
        scratch_shapes=[pltpu.VMEM((2 * tile, HEAD_W), BF16),
                        pltpu.VMEM((2, tile, 2 * tile), F32),
                        pltpu.VMEM((2, 1, 2 * tile), F32),
                        pltpu.VMEM((1, 2 * tile), F32),
                        pltpu.VMEM((1, 2 * tile), F32),
                        pltpu.VMEM((HEAD_W, 2 * tile), F32)],
        compiler_params=_cparams("parallel", "parallel", "arbitrary"),
        name="diff_attention",
    )(lam, far, qkv, qkv, qkv, bias_tab, subln_g)


def _rel_bucket(rel):
    nb = REL_BUCKETS // 2
    max_exact = nb // 2
    ret = jnp.where(rel > 0, nb, 0)
    n = jnp.abs(rel)
    nf = jnp.maximum(n, 1).astype(F32)
    large = max_exact + (jnp.log(nf / max_exact) / math.log(REL_MAX_DIST / max_exact)
                         * (nb - max_exact)).astype(jnp.int32)
    large = jnp.minimum(large, nb - 1)
    return ret + jnp.where(n < max_exact, n, large)


def _bias_kernel(rb_ref, bucket_ref, o_ref, *, tile):
    h = pl.program_id(0)
    half = REL_BUCKETS // 2
    ranges = ((half - 1, half), (0, half), (0, REL_BUCKETS), (half, REL_BUCKETS), (REL_BUCKETS - 1, REL_BUCKETS))
    rows = 4 * SUBLANES
    for t in range(BIAS_TILES):
        def body(i, carry, t=t):
            r0 = pl.multiple_of(i * rows, rows)
            bk = bucket_ref[t, pl.ds(r0, rows), :]
            acc = jnp.zeros(bk.shape, F32)
            for bkt in range(*ranges[t]):
                acc = jnp.where(bk == bkt, rb_ref[h, bkt], acc)
            o_ref[t, pl.ds(r0, rows), :] = acc
            return carry
        lax.fori_loop(0, tile // rows, body, 0)


def _bias_tables(rel_bias, tile):
    assert tile >= REL_MAX_DIST
    nh = rel_bias.shape[1]
    dd = jnp.arange(-2, 3, dtype=jnp.int32)[:, None, None]
    kk = jnp.arange(tile, dtype=jnp.int32)[None, :, None]
    qq = jnp.arange(tile, dtype=jnp.int32)[None, None, :]
    bucket = _rel_bucket(dd * tile + kk - qq)
    rb = jnp.transpose(rel_bias.astype(F32)) * LOG2E
    far = jnp.stack([rb[:, REL_BUCKETS // 2 - 1], rb[:, REL_BUCKETS - 1]], axis=1)
    tab = pl.pallas_call(
        functools.partial(_bias_kernel, tile=tile),
        out_shape=jax.ShapeDtypeStruct((nh, BIAS_TILES, tile, tile), F32),
        grid=(nh,),
        in_specs=[pl.BlockSpec(memory_space=pltpu.SMEM),
                  pl.BlockSpec((BIAS_TILES, tile, tile), lambda i: (0, 0, 0))],
        out_specs=pl.BlockSpec((None, BIAS_TILES, tile, tile), lambda i: (i, 0, 0, 0)),
        compiler_params=_cparams("parallel"),
        name="rel_bias_tiles",
    )(rb, bucket)
    return tab, far


def _ln(z, g, b):
    mu = jnp.mean(z, axis=-1, keepdims=True)
    zc = z - mu
    var = jnp.mean(zc * zc, axis=-1, keepdims=True)
    return zc * lax.rsqrt(var + LN_EPS) * g + b


def _route_epilogue(xn, wr_hi_ref, wr_lo_ref, br_ref, tri_ref, route_ref, counts_ref, base_ref):
    rows = xn.shape[0]
    x_hi, x_lo = _split_bf16(xn)
    nt = (((1,), (1,)), ((), ()))
    logits = (lax.dot_general(wr_hi_ref[...], x_hi, nt, preferred_element_type=F32)
              + lax.dot_general(wr_hi_ref[...], x_lo, nt, preferred_element_type=F32)
              + lax.dot_general(wr_lo_ref[...], x_hi, nt, preferred_element_type=F32))
    logits = logits + br_ref[...]

    lg = logits[0:FINE_ROW0]
    gmax = jnp.max(lg, axis=0, keepdims=True)
    pg_top = 1.0 / jnp.sum(jnp.exp(lg - gmax), axis=0, keepdims=True)
    gio = lax.broadcasted_iota(jnp.int32, lg.shape, 0).astype(F32)
    g_idx = jnp.min(jnp.where(lg == gmax, gio, float(FINE_ROW0)), axis=0, keepdims=True)

    sel = jnp.zeros((EXPERTS_PER_GROUP, rows), F32)
    for g in range(N_GROUPS):
        r0 = FINE_ROW0 + g * EXPERTS_PER_GROUP
        sel = jnp.where(g_idx == float(g), logits[r0:r0 + EXPERTS_PER_GROUP], sel)
    eio = lax.broadcasted_iota(jnp.int32, sel.shape, 0).astype(F32)
    v1 = jnp.max(sel, axis=0, keepdims=True)
    i1 = jnp.min(jnp.where(sel == v1, eio, float(EXPERTS_PER_GROUP)), axis=0, keepdims=True)
    rest = jnp.where(eio == i1, -jnp.inf, sel)
    v2 = jnp.max(rest, axis=0, keepdims=True)
    i2 = jnp.min(jnp.where(rest == v2, eio, float(EXPERTS_PER_GROUP)), axis=0, keepdims=True)
    e2 = jnp.exp(v2 - v1)
    den = 1.0 + e2
    gate1 = pg_top / den
    gate2 = pg_top * e2 / den
    eid1 = g_idx * float(EXPERTS_PER_GROUP) + i1
    eid2 = g_idx * float(EXPERTS_PER_GROUP) + i2

    xio = lax.broadcasted_iota(jnp.int32, (N_EXPERTS, rows), 0).astype(F32)
    oh1 = xio == eid1
    oh2 = xio == eid2
    cnt = jnp.where(oh1 | oh2, 1.0, 0.0)
    incl = jnp.dot(cnt.astype(BF16), tri_ref[...], preferred_element_type=F32)
    before = incl - cnt + base_ref[...]
    rank1 = jnp.sum(jnp.where(oh1, before, 0.0), axis=0, keepdims=True)
    rank2 = jnp.sum(jnp.where(oh2, before, 0.0), axis=0, keepdims=True)
    base_new = base_ref[...] + jnp.sum(cnt, axis=1, keepdims=True)
    base_ref[...] = base_new
    counts_ref[...] = jnp.broadcast_to(base_new, counts_ref.shape)
    rio = lax.broadcasted_iota(jnp.int32, route_ref.shape, 0)
    out = jnp.zeros(route_ref.shape, F32)
    for r, val in enumerate((eid1, eid2, gate1, gate2, rank1, rank2)):
        out = jnp.where(rio == r, val, out)
    route_ref[...] = out


def _proj_ln_kernel(a_ref, w_ref, x_ref, g_ref, b_ref, wr_hi_ref, wr_lo_ref, br_ref, tri_ref,
                    xn_ref, route_ref, counts_ref, base_ref):
    @pl.when(pl.program_id(0) == 0)
    def _():
        base_ref[...] = jnp.zeros(base_ref.shape, F32)

    h = jnp.dot(a_ref[...], w_ref[...], preferred_element_type=F32)
    xn = _ln(ALPHA * x_ref[...] + h, g_ref[...], b_ref[...])
    xn_ref[...] = xn
    _route_epilogue(xn, wr_hi_ref, wr_lo_ref, br_ref, tri_ref, route_ref, counts_ref, base_ref)


def _fft2_ln_kernel(a_ref, gbig_ref, mr_ref, mi_ref, bo_ref, x_ref, g_ref, b_ref,
                    wr_hi_ref, wr_lo_ref, br_ref, tri_ref,
                    xn_ref, route_ref, counts_ref, base_ref):
    @pl.when(pl.program_id(0) == 0)
    def _():
        base_ref[...] = jnp.zeros(base_ref.shape, F32)

    rows = xn_ref.shape[0]
    d = xn_ref.shape[1]
    a = a_ref[...].reshape(2 * rows, d)
    v = jnp.dot(gbig_ref[...], a, preferred_element_type=F32)
    h = (jnp.dot(v[:rows].astype(BF16), mr_ref[...], preferred_element_type=F32)
         + jnp.dot(v[rows:].astype(BF16), mi_ref[...], preferred_element_type=F32)
         + bo_ref[...])
    xn = _ln(ALPHA * x_ref[...].reshape(rows, d) + h, g_ref[...], b_ref[...])
    xn_ref[...] = xn
    _route_epilogue(xn, wr_hi_ref, wr_lo_ref, br_ref, tri_ref, route_ref, counts_ref, base_ref)


def _router_specs(d, rows):
    const2 = lambda i: (0, 0)
    ins = [pl.BlockSpec((1, d), const2), pl.BlockSpec((1, d), const2),
           pl.BlockSpec((ROUTER_ROWS, d), const2), pl.BlockSpec((ROUTER_ROWS, d), const2),
           pl.BlockSpec((ROUTER_ROWS, 1), const2), pl.BlockSpec((rows, rows), const2)]
    outs = [pl.BlockSpec((rows, d), lambda i: (i, 0)),
            pl.BlockSpec((SUBLANES, rows), lambda i: (0, i)),
            pl.BlockSpec((N_EXPERTS, LANES), const2)]
    return ins, outs


def _router_out_shapes(t, d):
    return (jax.ShapeDtypeStruct((t, d), F32),
            jax.ShapeDtypeStruct((SUBLANES, t), F32),
            jax.ShapeDtypeStruct((N_EXPERTS, LANES), F32))


def _proj_ln_router(att, wo_bf16, x2, ln_g, ln_b, router):
    t, d = x2.shape
    rows = ROW_TILE
    tail_in, outs = _router_specs(d, rows)
    return pl.pallas_call(
        _proj_ln_kernel,
        out_shape=_router_out_shapes(t, d),
        grid=(t // rows,),
        in_specs=[pl.BlockSpec((rows, d), lambda i: (i, 0)),
                  pl.BlockSpec((d, d), lambda i: (0, 0)),
                  pl.BlockSpec((rows, d), lambda i: (i, 0))] + tail_in,
        out_specs=outs,
        scratch_shapes=[pltpu.VMEM((N_EXPERTS, 1), F32)],
        compiler_params=_cparams("arbitrary"),
        name="proj_ln_router",
    )(att, wo_bf16, x2, ln_g, ln_b, *router)


def _fft1_kernel(f_ref, x_ref, o_ref):
    for j in range(x_ref.shape[1]):
        xj = x_ref[:, j, :].astype(BF16)
        o_ref[j] = jnp.dot(f_ref[...], xj, preferred_element_type=F32).astype(BF16)


def _fft_stage1(x3):
    b, s, d = x3.shape
    s1 = FFT_S1
    s2 = s // s1
    kb = FFT_KB
    k = np.arange(s2)
    ang = 2.0 * np.pi * ((k[:, None] * k[None, :]) % s2) / s2
    f = np.stack([np.cos(ang), -np.sin(ang)])
    f = f.reshape(2, s2 // kb, kb, s2).transpose(1, 0, 2, 3).reshape(2 * s2, s2)
    f = jnp.asarray(f, dtype=BF16)
    n1_tile = SUBLANES
    return pl.pallas_call(
        _fft1_kernel,
        out_shape=jax.ShapeDtypeStruct((b, s1, 2 * s2, d), BF16),
        grid=(b, s1 // n1_tile),
        in_specs=[pl.BlockSpec((2 * s2, s2), lambda bi, ni: (0, 0)),
                  pl.BlockSpec((None, s2, n1_tile, d), lambda bi, ni: (bi, 0, ni, 0))],
        out_specs=pl.BlockSpec((None, n1_tile, 2 * s2, d), lambda bi, ni: (bi, ni, 0, 0)),
        compiler_params=_cparams("parallel", "parallel"),
        name="fft_stage1",
    )(f, x3.reshape(b, s2, s1, d))


def _gbig_kernel(t_ref, o_ref):
    kb, m, _ = t_ref.shape
    big = kb * m
    ri = lax.broadcasted_iota(jnp.int32, (big, m), 0)
    ci = lax.broadcasted_iota(jnp.int32, (big, m), 1)
    rj = lax.broadcasted_iota(jnp.int32, (m, big), 0)
    cj = lax.broadcasted_iota(jnp.int32, (m, big), 1)
    acc = jnp.zeros((big, big), F32)
    for j in range(kb):
        place_rows = jnp.where(ri == kb * ci + j, 1.0, 0.0).astype(BF16)
        place_cols = jnp.where(cj == kb * rj + j, 1.0, 0.0).astype(BF16)
        rows = jnp.dot(place_rows, t_ref[j], preferred_element_type=F32).astype(BF16)
        acc = acc + jnp.dot(rows, place_cols, preferred_element_type=F32)
    o_ref[...] = acc.astype(BF16)


def _fft_stage2_matrix(s):
    s1 = FFT_S1
    s2 = s // s1
    kb = FFT_KB
    nblk = s2 // kb
    k2 = np.arange(s2)[:, None, None]
    k1 = np.arange(s1)[None, :, None]
    n1 = np.arange(s1)[None, None, :]
    ang = 2.0 * np.pi * ((k1 * n1 * s2 + k2 * n1) % s) / s
    gr, gi = np.cos(ang), -np.sin(ang)
    g = np.stack([np.stack([gr, -gi], axis=-1), np.stack([gi, gr], axis=-1)])
    t = g.reshape(2, nblk, kb, s1, 2 * s1).transpose(1, 2, 0, 3, 4).reshape(nblk, kb, 2 * s1, 2 * s1)
    m = 2 * kb * s1
    return pl.pallas_call(
        _gbig_kernel,
        out_shape=jax.ShapeDtypeStruct((nblk, m, m), BF16),
        grid=(nblk,),
        in_specs=[pl.BlockSpec((None, kb, 2 * s1, 2 * s1), lambda i: (i, 0, 0, 0))],
        out_specs=pl.BlockSpec((None, m, m), lambda i: (i, 0, 0)),
        compiler_params=_cparams("parallel"),
        name="fft_stage2_matrix",
    )(jnp.asarray(t, dtype=BF16))


def _fold_kernel(cs_hi_ref, cs_lo_ref, w_ref, o_ref):
    w_hi, w_lo = _split_bf16(w_ref[...])
    o_ref[...] = (jnp.dot(cs_hi_ref[...], w_hi, preferred_element_type=F32)
                  + jnp.dot(cs_hi_ref[...], w_lo, preferred_element_type=F32)
                  + jnp.dot(cs_lo_ref[...], w_hi, preferred_element_type=F32)).astype(BF16)


def _fold_channel_dft(w_o, s):
    d = w_o.shape[0]
    cg = d // FNET_GROUPS
    c = np.arange(cg)
    ang = 2.0 * np.pi * ((c[:, None] * c[None, :]) % cg) / cg
    scale = 1.0 / math.sqrt(s * cg)
    cs = jnp.asarray(np.stack([np.cos(ang), np.sin(ang)]) * scale, dtype=F32)
    cs_hi, cs_lo = _split_bf16(cs)
    out = pl.pallas_call(
        _fold_kernel,
        out_shape=jax.ShapeDtypeStruct((2, d, d), BF16),
        grid=(2, FNET_GROUPS),
        in_specs=[pl.BlockSpec((None, cg, cg), lambda ci, gi: (ci, 0, 0)),
                  pl.BlockSpec((None, cg, cg), lambda ci, gi: (ci, 0, 0)),
                  pl.BlockSpec((cg, d), lambda ci, gi: (gi, 0))],
        out_specs=pl.BlockSpec((None, cg, d), lambda ci, gi: (ci, gi, 0)),
        compiler_params=_cparams("parallel", "parallel"),
        name="fold_channel_dft",
    )(cs_hi, cs_lo, w_o.astype(F32))
    return out[0], out[1]


def _fft2_ln_router(a1, gbig, mr, mi, b_o, x3, ln_g, ln_b, router):
    b, s, d = x3.shape
    s1 = FFT_S1
    s2 = s // s1
    kb = FFT_KB
    nblk = s2 // kb
    rows = s1 * kb
    t = b * s
    a5 = a1.reshape(b, s1, nblk, 2 * kb, d)
    x5 = x3.reshape(b, s1, nblk, kb, d)
    tail_in, outs = _router_specs(d, rows)
    m = 2 * rows
    return pl.pallas_call(
        _fft2_ln_kernel,
        out_shape=_router_out_shapes(t, d),
        grid=(nblk * b,),
        in_specs=[pl.BlockSpec((None, s1, None, 2 * kb, d), lambda i: (i % b, 0, i // b, 0, 0)),
                  pl.BlockSpec((None, m, m), lambda i: (i // b, 0, 0)),
                  pl.BlockSpec((d, d), lambda i: (0, 0)),
                  pl.BlockSpec((d, d), lambda i: (0, 0)),
                  pl.BlockSpec((1, d), lambda i: (0, 0)),
                  pl.BlockSpec((None, s1, None, kb, d), lambda i: (i % b, 0, i // b, 0, 0))] + tail_in,
        out_specs=outs,
        scratch_shapes=[pltpu.VMEM((N_EXPERTS, 1), F32)],
        compiler_params=_cparams("arbitrary"),
        name="fft2_ln_router",
    )(a5, gbig, mr, mi, b_o, x5, ln_g, ln_b, *router)


def _dispatch_kernel(meta_ref, dest_ref, x_ref, xs_ref, zero_ref, sem, zsem, *, blk, nb):
    rows = x_ref.shape[0]

    @pl.when(pl.program_id(0) == 0)
    def _():
        zero_ref[...] = jnp.zeros(zero_ref.shape, F32)
        n_used = meta_ref[2 * N_EXPERTS]
        fills = [(meta_ref[N_EXPERTS + e] > 0, meta_ref[e] - blk) for e in range(N_EXPERTS)]
        fills += [(n_used + i < nb, (n_used + i) * blk) for i in range(N_EXPERTS)]

        def fill(row0):
            return pltpu.make_async_copy(zero_ref, xs_ref.at[pl.ds(pl.multiple_of(row0, blk), blk), :], zsem)

        for cond, row0 in fills:
            pl.when(cond)(lambda row0=row0: fill(row0).start())
        for cond, row0 in fills:
            pl.when(cond)(lambda row0=row0: fill(row0).wait())

    def body(r, carry):
        for kk in range(2):
            dst = dest_ref[0, 0, kk * rows + r]
            pltpu.make_async_copy(x_ref.at[pl.ds(r, 1), :], xs_ref.at[pl.ds(dst, 1), :], sem).start(priority=kk)
        return carry

    lax.fori_loop(0, rows, body, 0, unroll=8)
    for _ in range(2):
        pltpu.make_async_copy(x_ref, xs_ref.at[pl.ds(0, rows), :], sem).wait()


def _dispatch(xn, meta, dest_blocks, p_rows):
    t, d = xn.shape
    rows = DMA_TILE
    blk = EXPERT_BLOCK
    return pl.pallas_call(
        functools.partial(_dispatch_kernel, blk=blk, nb=p_rows // blk),
        out_shape=jax.ShapeDtypeStruct((p_rows, d), F32),
        grid=(t // rows,),
        in_specs=[pl.BlockSpec(memory_space=pltpu.SMEM),
                  pl.BlockSpec((1, 1, 2 * rows), lambda i: (i, 0, 0), memory_space=pltpu.SMEM),
                  pl.BlockSpec((rows, d), lambda i: (i, 0))],
        out_specs=pl.BlockSpec(memory_space=pl.ANY),
        scratch_shapes=[pltpu.VMEM((blk, d), F32), pltpu.SemaphoreType.DMA(()), pltpu.SemaphoreType.DMA(())],
        compiler_params=_cparams("arbitrary"),
        name="moe_dispatch",
    )(meta, dest_blocks, xn)


def _expert_kernel(bexp_ref, nused_ref, x_ref, w1_ref, w3_ref, w2_ref, o_ref, w13_bf, w2_bf, *, de):
    i = pl.program_id(0)

    @pl.when((i == 0) | (bexp_ref[i] != bexp_ref[jnp.maximum(i - 1, 0)]))
    def _():
        w13_bf[:, :de] = w1_ref[...].astype(BF16)
        w13_bf[:, de:] = w3_ref[...].astype(BF16)
        w2_bf[...] = w2_ref[...].astype(BF16)

    @pl.when(i < nused_ref[0])
    def _():
        xb = x_ref[...].astype(BF16)
        h = jnp.dot(xb, w13_bf[...], preferred_element_type=F32)
        a = h[:, :de]
        act = (a / (1.0 + jnp.exp(-a))) * h[:, de:]
        o_ref[...] = jnp.dot(act.astype(BF16), w2_bf[...], preferred_element_type=F32)

    @pl.when(i >= nused_ref[0])
    def _():
        o_ref[...] = jnp.zeros(o_ref.shape, F32)


def _expert_mlp(xs, w1, w3, w2, block_exp, n_used):
    p_rows, d = xs.shape
    de = w2.shape[1]
    nb = p_rows // EXPERT_BLOCK

    def in_row_map(i, bexp, nused):
        return (jnp.minimum(i, nused[0] - 1), 0)

    def w_map(i, bexp, nused):
        return (bexp[i], 0, 0)

    grid_spec = pltpu.PrefetchScalarGridSpec(
        num_scalar_prefetch=2,
        grid=(nb,),
        in_specs=[pl.BlockSpec((EXPERT_BLOCK, d), in_row_map),
                  pl.BlockSpec((None, d, de), w_map),
                  pl.BlockSpec((None, d, de), w_map),
                  pl.BlockSpec((None, de, d), w_map)],
        out_specs=pl.BlockSpec((EXPERT_BLOCK, d), lambda i, bexp, nused: (i, 0)),
        scratch_shapes=[pltpu.VMEM((d, 2 * de), BF16), pltpu.VMEM((de, d), BF16)],
    )
    return pl.pallas_call(
        functools.partial(_expert_kernel, de=de),
        out_shape=jax.ShapeDtypeStruct((p_rows, d), F32),
        grid_spec=grid_spec,
        compiler_params=_cparams("arbitrary"),
        name="expert_mlp",
    )(block_exp, n_used, xs, w1, w3, w2)


def _combine_ln_kernel(dest_ref, dest_next_ref, ys_ref, x_ref, gates_ref, g_ref, b_ref, o_ref, buf_ref, sem):
    i = pl.program_id(0)
    rows = x_ref.shape[0]
    slot = lax.rem(i, 2)

    def gather(d_ref, sl):
        def body(r, carry):
            for kk in range(2):
                src = d_ref[0, 0, kk * rows + r]
                pltpu.make_async_copy(ys_ref.at[pl.ds(src, 1), :], buf_ref.at[sl, kk, pl.ds(r, 1), :],
                                      sem.at[sl]).start(priority=kk)
            return carry
        lax.fori_loop(0, rows, body, 0, unroll=8)

    @pl.when(i == 0)
    def _():
        gather(dest_ref, 0)

    @pl.when(i + 1 < pl.num_programs(0))
    def _():
        gather(dest_next_ref, 1 - slot)

    for kk in range(2):
        pltpu.make_async_copy(ys_ref.at[pl.ds(0, rows), :], buf_ref.at[slot, kk], sem.at[slot]).wait()

    gates = gates_ref[...]
    m = gates[:, 0:1] * buf_ref[slot, 0] + gates[:, 1:2] * buf_ref[slot, 1]
    y = _ln(ALPHA * x_ref[...] + m, g_ref[...], b_ref[...])
    o_ref[...] = y.reshape(o_ref.shape)


def _combine_ln(ys, xn, gates, dest_blocks, ln_g, ln_b, out_shape, out_spec):
    t, d = xn.shape
    rows = DMA_TILE
    nsteps = t // rows
    return pl.pallas_call(
        _combine_ln_kernel,
        out_shape=out_shape,
        grid=(nsteps,),
        in_specs=[pl.BlockSpec((1, 1, 2 * rows), lambda i: (i, 0, 0), memory_space=pltpu.SMEM),
                  pl.BlockSpec((1, 1, 2 * rows), lambda i: (jnp.minimum(i + 1, nsteps - 1), 0, 0),
                               memory_space=pltpu.SMEM),
                  pl.BlockSpec(memory_space=pl.ANY),
                  pl.BlockSpec((rows, d), lambda i: (i, 0)),
                  pl.BlockSpec((rows, 2), lambda i: (i, 0)),
                  pl.BlockSpec((1, d), lambda i: (0, 0)),
                  pl.BlockSpec((1, d), lambda i: (0, 0))],
        out_specs=out_spec,
        scratch_shapes=[pltpu.VMEM((2, 2, rows, d), F32), pltpu.SemaphoreType.DMA((2,))],
        compiler_params=_cparams("arbitrary"),
        name="moe_combine_ln",
    )(dest_blocks, dest_blocks, ys, xn, gates, ln_g, ln_b)


def _moe_ln(xn, route, counts, experts_w, ln_g, ln_b, out_shape, out_spec):
    t, d = xn.shape
    blk = EXPERT_BLOCK
    nb = (2 * t) // blk + N_EXPERTS
    eid = route[0:2].astype(jnp.int32)
    rank = route[4:6].astype(jnp.int32)
    cnt = counts[:, 0].astype(jnp.int32)
    padded = (cnt + blk - 1) // blk * blk
    pends = jnp.cumsum(padded)
    pstarts = pends - padded
    experts = jnp.arange(N_EXPERTS, dtype=jnp.int32)
    dest = jnp.sum(jnp.where(eid[:, :, None] == experts, pstarts, 0), axis=-1) + rank
    nsteps = t // DMA_TILE
    dest_blocks = jnp.transpose(dest.reshape(2, nsteps, DMA_TILE), (1, 0, 2)).reshape(nsteps, 1, 2 * DMA_TILE)
    block_row0 = jnp.arange(nb, dtype=jnp.int32)[:, None] * blk
    block_exp = jnp.minimum(jnp.sum((pends[None, :] <= block_row0).astype(jnp.int32), axis=1), N_EXPERTS - 1)
    n_used = (pends[-1:] // blk).astype(jnp.int32)
    meta = jnp.concatenate([pends, padded, n_used]).astype(jnp.int32)
    gates = jnp.transpose(route[2:4])

    xs = _dispatch(xn, meta, dest_blocks, nb * blk)
    ys = _expert_mlp(xs, *experts_w, block_exp, n_used)
    return _combine_ln(ys, xn, gates, dest_blocks, ln_g, ln_b, out_shape, out_spec)


def _router_params(w_group, b_group, w_fine, b_fine, rows):
    d = w_group.shape[0]
    w = jnp.zeros((ROUTER_ROWS, d), F32)
    w = w.at[0:N_GROUPS].set(jnp.transpose(w_group).astype(F32))
    w = w.at[FINE_ROW0:FINE_ROW0 + N_EXPERTS].set(jnp.transpose(w_fine).astype(F32))
    bias = jnp.zeros((ROUTER_ROWS, 1), F32).at[N_GROUPS:FINE_ROW0, 0].set(NEG_BIG)
    bias = bias.at[0:N_GROUPS, 0].set(b_group.astype(F32))
    bias = bias.at[FINE_ROW0:FINE_ROW0 + N_EXPERTS, 0].set(b_fine.astype(F32))
    w_hi, w_lo = _split_bf16(w)
    tri = (jnp.arange(rows)[:, None] <= jnp.arange(rows)[None, :]).astype(BF16)
    return w_hi, w_lo, bias, tri


def _row(v):
    return v.reshape(1, -1).astype(F32)


def _trunk(x, p, shared):
    b, s, d = x.shape
    t = b * s
    h = d // HEAD_W
    assert s % ATT_TILE == 0 and t % ROW_TILE == 0 and s % (FFT_S1 * FFT_KB) == 0
    assert ROW_TILE == FFT_S1 * FFT_KB and ROW_TILE % DMA_TILE == 0
    flat_shape = jax.ShapeDtypeStruct((t, d), F32)
    flat_spec = pl.BlockSpec((DMA_TILE, d), lambda i: (i, 0))
    x2 = x.reshape(t, d)
    for i in range(DEPTH):
        j = i // 2
        if i % 2 == 0:
            qkv = _qkv_proj(x2, shared["wqkv"][j])
            att = _attention(qkv, *shared["bias"], shared["lam"][j], shared["subln_g"][j], b, s, d,
                             shared["lam_init"][j])
            xn, route, counts = _proj_ln_router(att, shared["wo"][j], x2, _row(p["ln1_g"][i]),
                                                _row(p["ln1_b"][i]), shared["router"][i])
            x2 = _moe_ln(xn, route, counts, shared["experts"][i],
                         _row(p["ln2_g"][i]), _row(p["ln2_b"][i]), flat_shape, flat_spec)
        else:
            x3 = x2.reshape(b, s, d)
            a1 = _fft_stage1(x3)
            mr, mi = shared["fold"][(j, s)]
            xn, route, counts = _fft2_ln_router(a1, shared["gbig"][s], mr, mi, _row(p["fnet_b_o"][j]), x3,
                                                _row(p["ln1_g"][i]), _row(p["ln1_b"][i]), shared["router"][i])
            nblk = s // (FFT_S1 * FFT_KB)
            halves = ROW_TILE // DMA_TILE
            k1_rows = FFT_S1 // halves
            out_shape = jax.ShapeDtypeStruct((b, FFT_S1, nblk, FFT_KB, d), F32)
            out_spec = pl.BlockSpec(
                (None, k1_rows, None, FFT_KB, d),
                lambda i2: ((i2 // halves) % b, i2 % halves, (i2 // halves) // b, 0, 0))
            y5 = _moe_ln(xn, route, counts, shared["experts"][i],
                         _row(p["ln2_g"][i]), _row(p["ln2_b"][i]), out_shape, out_spec)
            x2 = y5.reshape(t, d)
    return x2.reshape(b, s, d)


def kernel(x_prompt, x_sample, rel_bias, attn_w_qkv, attn_lambda_q1, attn_lambda_k1, attn_lambda_q2, attn_lambda_k2, attn_subln_g, attn_w_o, fnet_w_o, fnet_b_o, ln1_g, ln1_b, ln2_g, ln2_b, moe_w_group, moe_b_group, moe_w_fine, moe_b_fine, moe_w1, moe_w3, moe_w2):
    d = x_prompt.shape[-1]
    p = dict(fnet_b_o=fnet_b_o, ln1_g=ln1_g, ln1_b=ln1_b, ln2_g=ln2_g, ln2_b=ln2_b)
    n_attn = attn_w_qkv.shape[0]
    n_fnet = fnet_w_o.shape[0]
    seqs = sorted({x_prompt.shape[1], x_sample.shape[1]})

    qscale = jnp.concatenate([jnp.full((d,), HEAD_DIM ** -0.5 * LOG2E, F32), jnp.ones((2 * d,), F32)])
    shared = dict(
        wqkv=[(attn_w_qkv[j].astype(F32) * qscale).astype(BF16) for j in range(n_attn)],
        wo=[attn_w_o[j].astype(BF16) for j in range(n_attn)],
        subln_g=[_row(attn_subln_g[j]) for j in range(n_attn)],
        lam_init=[0.8 - 0.6 * math.exp(-0.3 * (2 * j)) for j in range(n_attn)],
        bias=_bias_tables(rel_bias, ATT_TILE),
        router=[_router_params(moe_w_group[i], moe_b_group[i], moe_w_fine[i], moe_b_fine[i], ROW_TILE)
                for i in range(DEPTH)],
        experts=[(moe_w1[i], moe_w3[i], moe_w2[i]) for i in range(DEPTH)],
        gbig={s: _fft_stage2_matrix(s) for s in seqs},
        fold={(j, s): _fold_channel_dft(fnet_w_o[j], s) for j in range(n_fnet) for s in seqs},
    )
    shared["lam"] = [
        (jnp.exp(jnp.sum(attn_lambda_q1[j].astype(F32) * attn_lambda_k1[j].astype(F32)))
         - jnp.exp(jnp.sum(attn_lambda_q2[j].astype(F32) * attn_lambda_k2[j].astype(F32)))
         + shared["lam_init"][j]).reshape(1).astype(F32)
        for j in range(n_attn)]

    return (_trunk(x_prompt, p, shared), _trunk(x_sample, p, shared))
```

```python
import functools
import math

import numpy as np
import jax
import jax.numpy as jnp
from jax import lax
from jax.experimental import pallas as pl
from jax.experimental.pallas import tpu as pltpu

F32 = jnp.float32
BF16 = jnp.bfloat16

HEAD_DIM = 64
HEAD_W = 2 * HEAD_DIM
REL_BUCKETS = 32
REL_MAX_DIST = 128
FNET_GROUPS = 4
N_GROUPS = 4
EXPERTS_PER_GROUP = 8
N_EXPERTS = N_GROUPS * EXPERTS_PER_GROUP
DEPTH = 2
ALPHA = (2 * DEPTH) ** 0.25
LN_EPS = 1e-5
LOG2E = 1.4426950408889634

LANES = 128
SUBLANES = 8
VMEM_LIMIT_BYTES = 52 * 1024 * 1024

ROW_TILE = 512
ATT_TILE = 512
BIAS_TILES = 5
EXPERT_BLOCK = 256
FFT_S1 = 64
FFT_KB = SUBLANES
ROUTER_ROWS = 48
FINE_ROW0 = 8
NEG_BIG = -1e30


def _cparams(*sem):
    return pltpu.CompilerParams(dimension_semantics=sem, vmem_limit_bytes=VMEM_LIMIT_BYTES)


def _split_bf16(a):
    hi = a.astype(BF16)
    lo = (a - hi.astype(F32)).astype(BF16)
    return hi, lo


def _qkv_kernel(x_ref, w_ref, o_ref, *, d):
    xb = x_ref[...].astype(BF16)
    for c in range(w_ref.shape[1] // d):
        o_ref[:, c * d:(c + 1) * d] = jnp.dot(
            xb, w_ref[:, c * d:(c + 1) * d], preferred_element_type=F32).astype(BF16)


def _qkv_proj(x2, w_bf16):
    t, d = x2.shape
    n = w_bf16.shape[1]
    return pl.pallas_call(
        functools.partial(_qkv_kernel, d=d),
        out_shape=jax.ShapeDtypeStruct((t, n), BF16),
        grid=(t // ROW_TILE,),
        in_specs=[pl.BlockSpec((ROW_TILE, d), lambda i: (i, 0)),
                  pl.BlockSpec((d, n), lambda i: (0, 0))],
        out_specs=pl.BlockSpec((ROW_TILE, n), lambda i: (i, 0)),
        compiler_params=_cparams("parallel"),
        name="qkv_proj",
    )(x2, w_bf16)


def _attn_kernel(lam_ref, far_ref, q_ref, k_ref, v_ref, bias_ref, g_ref, o_ref,
                 qbd_ref, s_ref, mx_ref, m_ref, l_ref, acc_ref, *, tile, nk, out_scale):
    hi = pl.program_id(1)
    qi = pl.program_id(2)
    qf = q_ref[...].astype(F32)
    lane = lax.broadcasted_iota(jnp.int32, qf.shape, 1)
    qbd_ref[0:tile, :] = jnp.where(lane < HEAD_DIM, qf, 0.0).astype(BF16)
    qbd_ref[tile:2 * tile, :] = jnp.where(lane >= HEAD_DIM, qf, 0.0).astype(BF16)
    m_ref[...] = jnp.full(m_ref.shape, NEG_BIG, F32)
    l_ref[...] = jnp.zeros(l_ref.shape, F32)
    acc_ref[...] = jnp.zeros(acc_ref.shape, F32)
    nt = (((1,), (1,)), ((), ()))
    tn = (((0,), (0,)), ((), ()))
    c_before = far_ref[hi, 0]
    c_after = far_ref[hi, 1]

    npairs = nk // 2 - 1
    near_lo = jnp.clip(qi // 2 - 1, 0, npairs)
    near_hi = jnp.clip(qi // 2 + 1, 0, npairs)

    def scores(kb, slot, with_bias):
        off = pl.multiple_of(kb * tile, tile)
        s = lax.dot_general(k_ref[pl.ds(off, tile), :], qbd_ref[...], nt, preferred_element_type=F32)
        if with_bias:
            b = bias_ref[jnp.clip(kb - qi, -2, 2) + 2]
            s = s + jnp.concatenate([b, b], axis=1)
        s_ref[slot] = s
        mx_ref[slot] = jnp.max(s, axis=0, keepdims=True)

    def softmax_pv(kb, slot):
        off = pl.multiple_of(kb * tile, tile)
        had_bias = (kb == 0) | (kb == nk - 1) | ((kb >= 2 * near_lo + 1) & (kb <= 2 * near_hi))
        shift = jnp.where(had_bias, 0.0, jnp.where(kb < qi, c_before, c_after))
        m_old = m_ref[...]
        m_new = jnp.maximum(m_old, mx_ref[slot] + shift)
        alpha = jnp.exp2(m_old - m_new)
        p = jnp.exp2(s_ref[slot] - (m_new - shift))
        l_ref[...] = alpha * l_ref[...] + jnp.sum(p, axis=0, keepdims=True)
        pv = lax.dot_general(v_ref[pl.ds(off, tile), :], p.astype(BF16), tn, preferred_element_type=F32)
        acc_ref[...] = acc_ref[...] * alpha + pv
        m_ref[...] = m_new

    def pairs(lo, hi_j, with_bias):
        def body(jj, carry):
            kb = 2 * jj
            scores(kb + 1, 1, with_bias)
            softmax_pv(kb, 0)
            scores(kb + 2, 0, with_bias)
            softmax_pv(kb + 1, 1)
            return carry
        lax.fori_loop(lo, hi_j, body, 0)

    scores(0, 0, True)
    pairs(0, near_lo, False)
    pairs(near_lo, near_hi, True)
    pairs(near_hi, npairs, False)
    scores(nk - 1, 1, True)
    softmax_pv(nk - 2, 0)
    softmax_pv(nk - 1, 1)

    lam = lam_ref[0]
    inv = 1.0 / l_ref[...]
    acc = acc_ref[...]
    o_t = acc[:, :tile] * inv[:, :tile] - lam * (acc[:, tile:] * inv[:, tile:])
    ms = jnp.mean(o_t * o_t, axis=0, keepdims=True)
    o_t = o_t * lax.rsqrt(ms + LN_EPS)
    o_ref[...] = ((o_t.T * g_ref[...]) * out_scale).astype(BF16)


def _attention(qkv, bias_tab, far, lam, subln_g, b, s, d, lam_init):
    h = d // HEAD_W
    tile = ATT_TILE
    nq = s // tile
    assert nq >= 2 and nq % 2 == 0
    kern = functools.partial(_attn_kernel, tile=tile, nk=nq, out_scale=1.0 - lam_init)
    return pl.pallas_call(
        kern,
        out_shape=jax.ShapeDtypeStruct((b * s, d), BF16),
        grid=(b, h, nq),
        in_specs=[
            pl.BlockSpec(memory_space=pltpu.SMEM),
            pl.BlockSpec(memory_space=pltpu.SMEM),
            pl.BlockSpec((tile, HEAD_W), lambda bi, hi, qi: (bi * nq + qi, hi)),
            pl.BlockSpec((s, HEAD_W), lambda bi, hi, qi: (bi, h + hi)),
            pl.BlockSpec((s, HEAD_W), lambda bi, hi, qi: (bi, 2 * h + hi)),
            pl.BlockSpec((None, BIAS_TILES, tile, tile), lambda bi, hi, qi: (hi, 0, 0, 0)),
            pl.BlockSpec((1, HEAD_W), lambda bi, hi, qi: (0, 0)),
        ],
        out_specs=pl.BlockSpec((tile, HEAD_W), lambda bi, hi, qi: (bi * nq + qi, hi)),
        scratch_shapes=[pltpu.VMEM((2 * tile, HEAD_W), BF16),
                        pltpu.VMEM((2, tile, 2 * tile), F32),
                        pltpu.VMEM((2, 1, 2 * tile), F32),
                        pltpu.VMEM((1, 2 * tile), F32),
                        pltpu.VMEM((1, 2 * tile), F32),
                        pltpu.VMEM((HEAD_W, 2 * tile), F32)],
        compiler_params=_cparams("parallel", "parallel", "arbitrary"),
        name="diff_attention",
    )(lam, far, qkv, qkv, qkv, bias_tab, subln_g)


def _rel_bucket(rel):
    nb = REL_BUCKETS // 2
    max_exact = nb // 2
    ret = jnp.where(rel > 0, nb, 0)
    n = jnp.abs(rel)
    nf = jnp.maximum(n, 1).astype(F32)
    large = max_exact + (jnp.log(nf / max_exact) / math.log(REL_MAX_DIST / max_exact)
                         * (nb - max_exact)).astype(jnp.int32)
    large = jnp.minimum(large, nb - 1)
    return ret + jnp.where(n < max_exact, n, large)


def _bias_kernel(rb_ref, bucket_ref, o_ref, *, tile):
    h = pl.program_id(0)
    half = REL_BUCKETS // 2
    ranges = ((half - 1, half), (0, half), (0, REL_BUCKETS), (half, REL_BUCKETS), (REL_BUCKETS - 1, REL_BUCKETS))
    rows = 4 * SUBLANES
    for t in range(BIAS_TILES):
        def body(i, carry, t=t):
            r0 = pl.multiple_of(i * rows, rows)
            bk = bucket_ref[t, pl.ds(r0, rows), :]
            acc = jnp.zeros(bk.shape, F32)
            for bkt in range(*ranges[t]):
                acc = jnp.where(bk == bkt, rb_ref[h, bkt], acc)
            o_ref[t, pl.ds(r0, rows), :] = acc
            return carry
        lax.fori_loop(0, tile // rows, body, 0)


def _bias_tables(rel_bias, tile):
    assert tile >= REL_MAX_DIST
    nh = rel_bias.shape[1]
    dd = jnp.arange(-2, 3, dtype=jnp.int32)[:, None, None]
    kk = jnp.arange(tile, dtype=jnp.int32)[None, :, None]
    qq = jnp.arange(tile, dtype=jnp.int32)[None, None, :]
    bucket = _rel_bucket(dd * tile + kk - qq)
    rb = jnp.transpose(rel_bias.astype(F32)) * LOG2E
    far = jnp.stack([rb[:, REL_BUCKETS // 2 - 1], rb[:, REL_BUCKETS - 1]], axis=1)
    tab = pl.pallas_call(
        functools.partial(_bias_kernel, tile=tile),
        out_shape=jax.ShapeDtypeStruct((nh, BIAS_TILES, tile, tile), F32),
        grid=(nh,),
        in_specs=[pl.BlockSpec(memory_space=pltpu.SMEM),
                  pl.BlockSpec((BIAS_TILES, tile, tile), lambda i: (0, 0, 0))],
        out_specs=pl.BlockSpec((None, BIAS_TILES, tile, tile), lambda i: (i, 0, 0, 0)),
        compiler_params=_cparams("parallel"),
        name="rel_bias_tiles",
    )(rb, bucket)
    return tab, far


def _ln(z, g, b):
    mu = jnp.mean(z, axis=-1, keepdims=True)
    zc = z - mu
    var = jnp.mean(zc * zc, axis=-1, keepdims=True)
    return zc * lax.rsqrt(var + LN_EPS) * g + b


def _route_epilogue(xn, wr_hi_ref, wr_lo_ref, br_ref, route_ref, counts_ref, base_ref):
    rows = xn.shape[0]
    x_hi, x_lo = _split_bf16(xn)
    nt = (((1,), (1,)), ((), ()))
    logits = (lax.dot_general(wr_hi_ref[...], x_hi, nt, preferred_element_type=F32)
              + lax.dot_general(wr_hi_ref[...], x_lo, nt, preferred_element_type=F32)
              + lax.dot_general(wr_lo_ref[...], x_hi, nt, preferred_element_type=F32))
    logits = logits + br_ref[...]

    lg = logits[0:FINE_ROW0]
    gmax = jnp.max(lg, axis=0, keepdims=True)
    pg_top = 1.0 / jnp.sum(jnp.exp(lg - gmax), axis=0, keepdims=True)
    gio = lax.broadcasted_iota(jnp.int32, lg.shape, 0).astype(F32)
    g_idx = jnp.min(jnp.where(lg == gmax, gio, float(FINE_ROW0)), axis=0, keepdims=True)

    sel = jnp.zeros((EXPERTS_PER_GROUP, rows), F32)
    for g in range(N_GROUPS):
        r0 = FINE_ROW0 + g * EXPERTS_PER_GROUP
        sel = jnp.where(g_idx == float(g), logits[r0:r0 + EXPERTS_PER_GROUP], sel)
    eio = lax.broadcasted_iota(jnp.int32, sel.shape, 0).astype(F32)
    v1 = jnp.max(sel, axis=0, keepdims=True)
    i1 = jnp.min(jnp.where(sel == v1, eio, float(EXPERTS_PER_GROUP)), axis=0, keepdims=True)
    rest = jnp.where(eio == i1, -jnp.inf, sel)
    v2 = jnp.max(rest, axis=0, keepdims=True)
    i2 = jnp.min(jnp.where(rest == v2, eio, float(EXPERTS_PER_GROUP)), axis=0, keepdims=True)
    e2 = jnp.exp(v2 - v1)
    den = 1.0 + e2
    gate1 = pg_top / den
    gate2 = pg_top * e2 / den
    eid1 = g_idx * float(EXPERTS_PER_GROUP) + i1
    eid2 = g_idx * float(EXPERTS_PER_GROUP) + i2

    xio = lax.broadcasted_iota(jnp.int32, (N_EXPERTS, rows), 0).astype(F32)
    cnt = jnp.where((xio == eid1) | (xio == eid2), 1.0, 0.0)
    base_new = base_ref[...] + jnp.sum(cnt, axis=1, keepdims=True)
    base_ref[...] = base_new
    counts_ref[...] = jnp.broadcast_to(base_new, counts_ref.shape)
    rio = lax.broadcasted_iota(jnp.int32, route_ref.shape, 0)
    out = jnp.zeros(route_ref.shape, F32)
    for r, val in enumerate((eid1, eid2, gate1, gate2)):
        out = jnp.where(rio == r, val, out)
    route_ref[...] = out


def _store_rows(xn, xn_ref, xt_ref):
    xn_ref[...] = xn
    xt_ref[...] = _tiles_from_rows(xn)


def _proj_ln_kernel(a_ref, w_ref, x_ref, g_ref, b_ref, wr_hi_ref, wr_lo_ref, br_ref,
                    xn_ref, xt_ref, route_ref, counts_ref, base_ref):
    @pl.when(pl.program_id(0) == 0)
    def _():
        base_ref[...] = jnp.zeros(base_ref.shape, F32)

    h = jnp.dot(a_ref[...], w_ref[...], preferred_element_type=F32)
    xn = _ln(ALPHA * x_ref[...] + h, g_ref[...], b_ref[...])
    _store_rows(xn, xn_ref, xt_ref)
    _route_epilogue(xn, wr_hi_ref, wr_lo_ref, br_ref, route_ref, counts_ref, base_ref)


def _fft2_ln_kernel(a_ref, gbig_ref, mr_ref, mi_ref, bo_ref, x_ref, g_ref, b_ref,
                    wr_hi_ref, wr_lo_ref, br_ref,
                    xn_ref, xt_ref, route_ref, counts_ref, base_ref):
    @pl.when(pl.program_id(0) == 0)
    def _():
        base_ref[...] = jnp.zeros(base_ref.shape, F32)

    rows = xn_ref.shape[0]
    d = xn_ref.shape[1]
    a = a_ref[...].reshape(2 * rows, d)
    v = jnp.dot(gbig_ref[...], a, preferred_element_type=F32)
    h = (jnp.dot(v[:rows].astype(BF16), mr_ref[...], preferred_element_type=F32)
         + jnp.dot(v[rows:].astype(BF16), mi_ref[...], preferred_element_type=F32)
         + bo_ref[...])
    xn = _ln(ALPHA * x_ref[...].reshape(rows, d) + h, g_ref[...], b_ref[...])
    _store_rows(xn, xn_ref, xt_ref)
    _route_epilogue(xn, wr_hi_ref, wr_lo_ref, br_ref, route_ref, counts_ref, base_ref)


def _router_specs(d, rows):
    const2 = lambda i: (0, 0)
    ins = [pl.BlockSpec((1, d), const2), pl.BlockSpec((1, d), const2),
           pl.BlockSpec((ROUTER_ROWS, d), const2), pl.BlockSpec((ROUTER_ROWS, d), const2),
           pl.BlockSpec((ROUTER_ROWS, 1), const2)]
    outs = [pl.BlockSpec((rows, d), lambda i: (i, 0)),
            pl.BlockSpec((rows, d // LANES, LANES), lambda i: (i, 0, 0)),
            pl.BlockSpec((SUBLANES, rows), lambda i: (0, i)),
            pl.BlockSpec((N_EXPERTS, LANES), const2)]
    return ins, outs


def _router_out_shapes(t, d):
    return (jax.ShapeDtypeStruct((t, d), F32),
            jax.ShapeDtypeStruct((t, d // LANES, LANES), F32),
            jax.ShapeDtypeStruct((SUBLANES, t), F32),
            jax.ShapeDtypeStruct((N_EXPERTS, LANES), F32))


def _proj_ln_router(att, wo_bf16, x2, ln_g, ln_b, router):
    t, d = x2.shape
    rows = ROW_TILE
    tail_in, outs = _router_specs(d, rows)
    return pl.pallas_call(
        _proj_ln_kernel,
        out_shape=_router_out_shapes(t, d),
        grid=(t // rows,),
        in_specs=[pl.BlockSpec((rows, d), lambda i: (i, 0)),
                  pl.BlockSpec((d, d), lambda i: (0, 0)),
                  pl.BlockSpec((rows, d), lambda i: (i, 0))] + tail_in,
        out_specs=outs,
        scratch_shapes=[pltpu.VMEM((N_EXPERTS, 1), F32)],
        compiler_params=_cparams("arbitrary"),
        name="proj_ln_router",
    )(att, wo_bf16, x2, ln_g, ln_b, *router)


def _fft1_kernel(f_ref, x_ref, o_ref):
    for j in range(x_ref.shape[1]):
        xj = x_ref[:, j, :].astype(BF16)
        o_ref[j] = jnp.dot(f_ref[...], xj, preferred_element_type=F32).astype(BF16)


def _fft_stage1(x3):
    b, s, d = x3.shape
    s1 = FFT_S1
    s2 = s // s1
    kb = FFT_KB
    k = np.arange(s2)
    ang = 2.0 * np.pi * ((k[:, None] * k[None, :]) % s2) / s2
    f = np.stack([np.cos(ang), -np.sin(ang)])
    f = f.reshape(2, s2 // kb, kb, s2).transpose(1, 0, 2, 3).reshape(2 * s2, s2)
    f = jnp.asarray(f, dtype=BF16)
    n1_tile = SUBLANES
    return pl.pallas_call(
        _fft1_kernel,
        out_shape=jax.ShapeDtypeStruct((b, s1, 2 * s2, d), BF16),
        grid=(b, s1 // n1_tile),
        in_specs=[pl.BlockSpec((2 * s2, s2), lambda bi, ni: (0, 0)),
                  pl.BlockSpec((None, s2, n1_tile, d), lambda bi, ni: (bi, 0, ni, 0))],
        out_specs=pl.BlockSpec((None, n1_tile, 2 * s2, d), lambda bi, ni: (bi, ni, 0, 0)),
        compiler_params=_cparams("parallel", "parallel"),
        name="fft_stage1",
    )(f, x3.reshape(b, s2, s1, d))


def _gbig_kernel(t_ref, o_ref):
    kb, m, _ = t_ref.shape
    big = kb * m
    ri = lax.broadcasted_iota(jnp.int32, (big, m), 0)
    ci = lax.broadcasted_iota(jnp.int32, (big, m), 1)
    rj = lax.broadcasted_iota(jnp.int32, (m, big), 0)
    cj = lax.broadcasted_iota(jnp.int32, (m, big), 1)
    acc = jnp.zeros((big, big), F32)
    for j in range(kb):
        place_rows = jnp.where(ri == kb * ci + j, 1.0, 0.0).astype(BF16)
        place_cols = jnp.where(cj == kb * rj + j, 1.0, 0.0).astype(BF16)
        rows = jnp.dot(place_rows, t_ref[j], preferred_element_type=F32).astype(BF16)
        acc = acc + jnp.dot(rows, place_cols, preferred_element_type=F32)
    o_ref[...] = acc.astype(BF16)


def _fft_stage2_matrix(s):
    s1 = FFT_S1
    s2 = s // s1
    kb = FFT_KB
    nblk = s2 // kb
    k2 = np.arange(s2)[:, None, None]
    k1 = np.arange(s1)[None, :, None]
    n1 = np.arange(s1)[None, None, :]
    ang = 2.0 * np.pi * ((k1 * n1 * s2 + k2 * n1) % s) / s
    gr, gi = np.cos(ang), -np.sin(ang)
    g = np.stack([np.stack([gr, -gi], axis=-1), np.stack([gi, gr], axis=-1)])
    t = g.reshape(2, nblk, kb, s1, 2 * s1).transpose(1, 2, 0, 3, 4).reshape(nblk, kb, 2 * s1, 2 * s1)
    m = 2 * kb * s1
    return pl.pallas_call(
        _gbig_kernel,
        out_shape=jax.ShapeDtypeStruct((nblk, m, m), BF16),
        grid=(nblk,),
        in_specs=[pl.BlockSpec((None, kb, 2 * s1, 2 * s1), lambda i: (i, 0, 0, 0))],
        out_specs=pl.BlockSpec((None, m, m), lambda i: (i, 0, 0)),
        compiler_params=_cparams("parallel"),
        name="fft_stage2_matrix",
    )(jnp.asarray(t, dtype=BF16))


def _fold_kernel(cs_hi_ref, cs_lo_ref, w_ref, o_ref):
    w_hi, w_lo = _split_bf16(w_ref[...])
    o_ref[...] = (jnp.dot(cs_hi_ref[...], w_hi, preferred_element_type=F32)
                  + jnp.dot(cs_hi_ref[...], w_lo, preferred_element_type=F32)
                  + jnp.dot(cs_lo_ref[...], w_hi, preferred_element_type=F32)).astype(BF16)


def _fold_channel_dft(w_o, s):
    d = w_o.shape[0]
    cg = d // FNET_GROUPS
    c = np.arange(cg)
    ang = 2.0 * np.pi * ((c[:, None] * c[None, :]) % cg) / cg
    scale = 1.0 / math.sqrt(s * cg)
    cs = jnp.asarray(np.stack([np.cos(ang), np.sin(ang)]) * scale, dtype=F32)
    cs_hi, cs_lo = _split_bf16(cs)
    out = pl.pallas_call(
        _fold_kernel,
        out_shape=jax.ShapeDtypeStruct((2, d, d), BF16),
        grid=(2, FNET_GROUPS),
        in_specs=[pl.BlockSpec((None, cg, cg), lambda ci, gi: (ci, 0, 0)),
                  pl.BlockSpec((None, cg, cg), lambda ci, gi: (ci, 0, 0)),
                  pl.BlockSpec((cg, d), lambda ci, gi: (gi, 0))],
        out_specs=pl.BlockSpec((None, cg, d), lambda ci, gi: (ci, gi, 0)),
        compiler_params=_cparams("parallel", "parallel"),
        name="fold_channel_dft",
    )(cs_hi, cs_lo, w_o.astype(F32))
    return out[0], out[1]


def _fft2_ln_router(a1, gbig, mr, mi, b_o, x3, ln_g, ln_b, router):
    b, s, d = x3.shape
    s1 = FFT_S1
    s2 = s // s1
    kb = FFT_KB
    nblk = s2 // kb
    rows = s1 * kb
    t = b * s
    a5 = a1.reshape(b, s1, nblk, 2 * kb, d)
    x5 = x3.reshape(b, s1, nblk, kb, d)
    tail_in, outs = _router_specs(d, rows)
    m = 2 * rows
    return pl.pallas_call(
        _fft2_ln_kernel,
        out_shape=_router_out_shapes(t, d),
        grid=(nblk * b,),
        in_specs=[pl.BlockSpec((None, s1, None, 2 * kb, d), lambda i: (i % b, 0, i // b, 0, 0)),
                  pl.BlockSpec((None, m, m), lambda i: (i // b, 0, 0)),
                  pl.BlockSpec((d, d), lambda i: (0, 0)),
                  pl.BlockSpec((d, d), lambda i: (0, 0)),
                  pl.BlockSpec((1, d), lambda i: (0, 0)),
                  pl.BlockSpec((None, s1, None, kb, d), lambda i: (i % b, 0, i // b, 0, 0))] + tail_in,
        out_specs=outs,
        scratch_shapes=[pltpu.VMEM((N_EXPERTS, 1), F32)],
        compiler_params=_cparams("arbitrary"),
        name="fft2_ln_router",
    )(a5, gbig, mr, mi, b_o, x5, ln_g, ln_b, *router)


def _rows_from_tiles(ref, *lead):
    xt = pltpu.einshape("rcl->crl", ref[lead] if lead else ref[...])
    return jnp.concatenate([xt[c] for c in range(xt.shape[0])], axis=1)


def _tiles_from_rows(y):
    yt = jnp.stack([y[:, c * LANES:(c + 1) * LANES] for c in range(y.shape[1] // LANES)])
    return pltpu.einshape("crl->rcl", yt)


def _expert_kernel(bexp_ref, nused_ref, src_ref, src_next_ref, dst_ref, xn_ref, w1_ref, w3_ref, w2_ref, out_ref,
                   xbuf, ybuf, w13_bf, w2_bf, gsem, ssem, *, de):
    i = pl.program_id(0)
    last = pl.num_programs(0) - 1
    n_used = nused_ref[0]
    slot = lax.rem(i, 2)
    blk = xbuf.shape[1]
    half = blk // 2

    def gather(idx_ref, sl):
        def body(r, carry):
            for pr in range(2):
                row = pr * half + r
                pltpu.make_async_copy(xn_ref.at[idx_ref[0, 0, row]], xbuf.at[sl, row],
                                      gsem.at[sl]).start(priority=pr)
            return carry
        lax.fori_loop(0, half, body, 0, unroll=8)

    def gather_wait(sl):
        pltpu.make_async_copy(xn_ref.at[pl.ds(0, blk)], xbuf.at[sl], gsem.at[sl]).wait()

    def scatter_wait(sl):
        pltpu.make_async_copy(ybuf.at[sl], out_ref.at[pl.ds(0, blk)], ssem.at[sl]).wait()

    @pl.when(i == 0)
    def _():
        gather(src_ref, 0)

    @pl.when(i + 1 < n_used)
    def _():
        gather(src_next_ref, 1 - slot)

    @pl.when((i == 0) | (bexp_ref[i] != bexp_ref[jnp.maximum(i - 1, 0)]))
    def _():
        w13_bf[:, :de] = w1_ref[...].astype(BF16)
        w13_bf[:, de:] = w3_ref[...].astype(BF16)
        w2_bf[...] = w2_ref[...].astype(BF16)

    @pl.when(i >= 2)
    def _():
        scatter_wait(slot)

    @pl.when(i < n_used)
    def _():
        gather_wait(slot)
        xb = _rows_from_tiles(xbuf, slot).astype(BF16)
        h = jnp.dot(xb, w13_bf[...], preferred_element_type=F32)
        a = h[:, :de]
        act = (a / (1.0 + jnp.exp(-a))) * h[:, de:]
        y = jnp.dot(act.astype(BF16), w2_bf[...], preferred_element_type=F32)
        ybuf[slot] = _tiles_from_rows(y)

        def body(r, carry):
            for pr in range(2):
                row = pr * half + r
                pltpu.make_async_copy(ybuf.at[slot, row], out_ref.at[dst_ref[0, 0, row]],
                                      ssem.at[slot]).start(priority=pr)
            return carry
        lax.fori_loop(0, half, body, 0, unroll=8)

    @pl.when(i >= n_used)
    def _():
        ybuf[slot] = jnp.zeros(ybuf.shape[1:], F32)
        pltpu.make_async_copy(ybuf.at[slot], out_ref.at[pl.ds(dst_ref[0, 0, 0], blk)], ssem.at[slot]).start()

    @pl.when(i == last)
    def _():
        @pl.when(i >= 1)
        def _():
            scatter_wait(1 - slot)
        scatter_wait(slot)


def _expert_mlp(xn_tiles, src_tok, dst_row, layer, w1, w3, w2, block_exp, n_used):
    nb, _, blk = src_tok.shape
    tile_rows = xn_tiles.shape[1]
    d = tile_rows * LANES
    de = w2.shape[2]

    def w_map(i, bexp, nused):
        return (layer, bexp[i], 0, 0)

    smem_block = functools.partial(pl.BlockSpec, (1, 1, blk), memory_space=pltpu.SMEM)
    grid_spec = pltpu.PrefetchScalarGridSpec(
        num_scalar_prefetch=2,
        grid=(nb,),
        in_specs=[smem_block(lambda i, bexp, nused: (i, 0, 0)),
                  smem_block(lambda i, bexp, nused: (jnp.minimum(i + 1, nb - 1), 0, 0)),
                  smem_block(lambda i, bexp, nused: (i, 0, 0)),
                  pl.BlockSpec(memory_space=pl.ANY),
                  pl.BlockSpec((None, None, d, de), w_map),
                  pl.BlockSpec((None, None, d, de), w_map),
                  pl.BlockSpec((None, None, de, d), w_map)],
        out_specs=pl.BlockSpec(memory_space=pl.ANY),
        scratch_shapes=[pltpu.VMEM((2, blk, tile_rows, LANES), F32), pltpu.VMEM((2, blk, tile_rows, LANES), F32),
                        pltpu.VMEM((d, 2 * de), BF16), pltpu.VMEM((de, d), BF16),
                        pltpu.SemaphoreType.DMA((2,)), pltpu.SemaphoreType.DMA((2,))],
    )
    return pl.pallas_call(
        functools.partial(_expert_kernel, de=de),
        out_shape=jax.ShapeDtypeStruct((nb * blk, tile_rows, LANES), F32),
        grid_spec=grid_spec,
        compiler_params=_cparams("arbitrary"),
        name="expert_mlp",
    )(block_exp, n_used, src_tok, src_tok, dst_row, xn_tiles, w1, w3, w2)


def _combine_ln_kernel(y1_ref, y2_ref, x_ref, gates_ref, g_ref, b_ref, o_ref):
    gates = gates_ref[...]
    m = gates[:, 0:1] * _rows_from_tiles(y1_ref) + gates[:, 1:2] * _rows_from_tiles(y2_ref)
    y = _ln(ALPHA * x_ref[...] + m, g_ref[...], b_ref[...])
    o_ref[...] = y.reshape(o_ref.shape)


def _combine_ln(ys, xn, gates, ln_g, ln_b, out_shape, out_spec):
    t, d = xn.shape
    rows = ROW_TILE
    nsteps = t // rows
    tile_rows = ys.shape[1]
    return pl.pallas_call(
        _combine_ln_kernel,
        out_shape=out_shape,
        grid=(nsteps,),
        in_specs=[pl.BlockSpec((rows, tile_rows, LANES), lambda i: (i, 0, 0)),
                  pl.BlockSpec((rows, tile_rows, LANES), lambda i: (nsteps + i, 0, 0)),
                  pl.BlockSpec((rows, d), lambda i: (i, 0)),
                  pl.BlockSpec((rows, 2), lambda i: (i, 0)),
                  pl.BlockSpec((1, d), lambda i: (0, 0)),
                  pl.BlockSpec((1, d), lambda i: (0, 0))],
        out_specs=out_spec,
        compiler_params=_cparams("parallel"),
        name="moe_combine_ln",
    )(ys, ys, xn, gates, ln_g, ln_b)


def _moe_ln(xn, xn_tiles, route, counts, experts_w, ln_g, ln_b, out_shape, out_spec):
    t, d = xn.shape
    blk = EXPERT_BLOCK
    nb = (2 * t) // blk + N_EXPERTS
    p_rows = nb * blk
    n_dummy = p_rows - 2 * t
    eid = route[0:2].astype(jnp.int32).reshape(2 * t)
    cnt = counts[:, 0].astype(jnp.int32)
    padded = (cnt + blk - 1) // blk * blk
    pends = jnp.cumsum(padded)
    pad_ends = jnp.cumsum(padded - cnt)
    dummy = jnp.arange(n_dummy, dtype=jnp.int32) - (p_rows - pends[-1])
    dummy_expert = jnp.where(dummy < 0, N_EXPERTS,
                             jnp.sum((pad_ends[None, :] <= dummy[:, None]).astype(jnp.int32), axis=1))
    group = jnp.concatenate([2 * eid, 2 * dummy_expert + 1])
    slot_asg = lax.sort(group * p_rows + jnp.arange(p_rows, dtype=jnp.int32)) % p_rows
    src_tok = jnp.where(slot_asg < t, slot_asg, jnp.where(slot_asg < 2 * t, slot_asg - t, 0))
    block_row0 = jnp.arange(nb, dtype=jnp.int32)[:, None] * blk
    block_exp = jnp.minimum(jnp.sum((pends[None, :] <= block_row0).astype(jnp.int32), axis=1), N_EXPERTS - 1)
    n_used = (pends[-1:] // blk).astype(jnp.int32)
    gates = jnp.transpose(route[2:4])

    ys = _expert_mlp(xn_tiles, src_tok.reshape(nb, 1, blk), slot_asg.reshape(nb, 1, blk), *experts_w,
                     block_exp, n_used)
    return _combine_ln(ys, xn, gates, ln_g, ln_b, out_shape, out_spec)


def _router_params(w_group, b_group, w_fine, b_fine, rows):
    d = w_group.shape[0]
    w = jnp.zeros((ROUTER_ROWS, d), F32)
    w = w.at[0:N_GROUPS].set(jnp.transpose(w_group).astype(F32))
    w = w.at[FINE_ROW0:FINE_ROW0 + N_EXPERTS].set(jnp.transpose(w_fine).astype(F32))
    bias = jnp.zeros((ROUTER_ROWS, 1), F32).at[N_GROUPS:FINE_ROW0, 0].set(NEG_BIG)
    bias = bias.at[0:N_GROUPS, 0].set(b_group.astype(F32))
    bias = bias.at[FINE_ROW0:FINE_ROW0 + N_EXPERTS, 0].set(b_fine.astype(F32))
    w_hi, w_lo = _split_bf16(w)
    return w_hi, w_lo, bias


def _row(v):
    return v.reshape(1, -1).astype(F32)


def _trunk(x, p, shared):
    b, s, d = x.shape
    t = b * s
    h = d // HEAD_W
    assert s % ATT_TILE == 0 and t % ROW_TILE == 0 and s % (FFT_S1 * FFT_KB) == 0
    assert ROW_TILE == FFT_S1 * FFT_KB
    flat_shape = jax.ShapeDtypeStruct((t, d), F32)
    flat_spec = pl.BlockSpec((ROW_TILE, d), lambda i: (i, 0))
    x2 = x.reshape(t, d)
    for i in range(DEPTH):
        j = i // 2
        if i % 2 == 0:
            qkv = _qkv_proj(x2, shared["wqkv"][j])
            att = _attention(qkv, *shared["bias"], shared["lam"][j], shared["subln_g"][j], b, s, d,
                             shared["lam_init"][j])
            xn, xt, route, counts = _proj_ln_router(att, shared["wo"][j], x2, _row(p["ln1_g"][i]),
                                                    _row(p["ln1_b"][i]), shared["router"][i])
            x2 = _moe_ln(xn, xt, route, counts, shared["experts"][i],
                         _row(p["ln2_g"][i]), _row(p["ln2_b"][i]), flat_shape, flat_spec)
        else:
            x3 = x2.reshape(b, s, d)
            a1 = _fft_stage1(x3)
            mr, mi = shared["fold"][(j, s)]
            xn, xt, route, counts = _fft2_ln_router(a1, shared["gbig"][s], mr, mi, _row(p["fnet_b_o"][j]), x3,
                                                    _row(p["ln1_g"][i]), _row(p["ln1_b"][i]),
                                                    shared["router"][i])
            nblk = s // (FFT_S1 * FFT_KB)
            out_shape = jax.ShapeDtypeStruct((b, FFT_S1, nblk, FFT_KB, d), F32)
            out_spec = pl.BlockSpec((None, FFT_S1, None, FFT_KB, d), lambda i2: (i2 % b, 0, i2 // b, 0, 0))
            y5 = _moe_ln(xn, xt, route, counts, shared["experts"][i],
                         _row(p["ln2_g"][i]), _row(p["ln2_b"][i]), out_shape, out_spec)
            x2 = y5.reshape(t, d)
    return x2.reshape(b, s, d)


def kernel(x_prompt, x_sample, rel_bias, attn_w_qkv, attn_lambda_q1, attn_lambda_k1, attn_lambda_q2, attn_lambda_k2, attn_subln_g, attn_w_o, fnet_w_o, fnet_b_o, ln1_g, ln1_b, ln2_g, ln2_b, moe_w_group, moe_b_group, moe_w_fine, moe_b_fine, moe_w1, moe_w3, moe_w2):
    d = x_prompt.shape[-1]
    p = dict(fnet_b_o=fnet_b_o, ln1_g=ln1_g, ln1_b=ln1_b, ln2_g=ln2_g, ln2_b=ln2_b)
    n_attn = attn_w_qkv.shape[0]
    n_fnet = fnet_w_o.shape[0]
    seqs = sorted({x_prompt.shape[1], x_sample.shape[1]})

    qscale = jnp.concatenate([jnp.full((d,), HEAD_DIM ** -0.5 * LOG2E, F32), jnp.ones((2 * d,), F32)])
    shared = dict(
        wqkv=[(attn_w_qkv[j].astype(F32) * qscale).astype(BF16) for j in range(n_attn)],
        wo=[attn_w_o[j].astype(BF16) for j in range(n_attn)],
        subln_g=[_row(attn_subln_g[j]) for j in range(n_attn)],
        lam_init=[0.8 - 0.6 * math.exp(-0.3 * (2 * j)) for j in range(n_attn)],
        bias=_bias_tables(rel_bias, ATT_TILE),
        router=[_router_params(moe_w_group[i], moe_b_group[i], moe_w_fine[i], moe_b_fine[i], ROW_TILE)
                for i in range(DEPTH)],
        experts=[(i, moe_w1, moe_w3, moe_w2) for i in range(DEPTH)],
        gbig={s: _fft_stage2_matrix(s) for s in seqs},
        fold={(j, s): _fold_channel_dft(fnet_w_o[j], s) for j in range(n_fnet) for s in seqs},
    )
    shared["lam"] = [
        (jnp.exp(jnp.sum(attn_lambda_q1[j].astype(F32) * attn_lambda_k1[j].astype(F32)))
         - jnp.exp(jnp.sum(attn_lambda_q2[j].astype(F32) * attn_lambda_k2[j].astype(F32)))
         + shared["lam_init"][j]).reshape(1).astype(F32)
        for j in range(n_attn)]

    return (_trunk(x_prompt, p, shared), _trunk(x_sample, p, shared))
```

```python
import functools
import math

import numpy as np
import jax
import jax.numpy as jnp
from jax import lax
from jax.experimental import pallas as pl
from jax.experimental.pallas import tpu as pltpu

F32 = jnp.float32
BF16 = jnp.bfloat16

HEAD_DIM = 64
HEAD_W = 2 * HEAD_DIM
REL_BUCKETS = 32
REL_MAX_DIST = 128
FNET_GROUPS = 4
N_GROUPS = 4
EXPERTS_PER_GROUP = 8
N_EXPERTS = N_GROUPS * EXPERTS_PER_GROUP
DEPTH = 2
ALPHA = (2 * DEPTH) ** 0.25
LN_EPS = 1e-5
LOG2E = 1.4426950408889634

LANES = 128
SUBLANES = 8
VMEM_LIMIT_BYTES = 52 * 1024 * 1024

ROW_TILE = 512
ATT_TILE = 512
BIAS_TILES = 5
EXPERT_BLOCK = 256
FFT_S1 = 64
FFT_KB = SUBLANES
ROUTER_ROWS = 48
FINE_ROW0 = 8
NEG_BIG = -1e30


def _cparams(*sem):
    return pltpu.CompilerParams(dimension_semantics=sem, vmem_limit_bytes=VMEM_LIMIT_BYTES)


def _split_bf16(a):
    hi = a.astype(BF16)
    lo = (a - hi.astype(F32)).astype(BF16)
    return hi, lo


def _qkv_kernel(x_ref, w_ref, o_ref, *, d):
    xb = x_ref[...].astype(BF16)
    for c in range(w_ref.shape[1] // d):
        o_ref[:, c * d:(c + 1) * d] = jnp.dot(
            xb, w_ref[:, c * d:(c + 1) * d], preferred_element_type=F32).astype(BF16)


def _qkv_proj(x2, w_bf16):
    t, d = x2.shape
    n = w_bf16.shape[1]
    return pl.pallas_call(
        functools.partial(_qkv_kernel, d=d),
        out_shape=jax.ShapeDtypeStruct((t, n), BF16),
        grid=(t // ROW_TILE,),
        in_specs=[pl.BlockSpec((ROW_TILE, d), lambda i: (i, 0)),
                  pl.BlockSpec((d, n), lambda i: (0, 0))],
        out_specs=pl.BlockSpec((ROW_TILE, n), lambda i: (i, 0)),
        compiler_params=_cparams("parallel"),
        name="qkv_proj",
    )(x2, w_bf16)


def _attn_kernel(lam_ref, far_ref, q_ref, k_ref, v_ref, bias_ref, g_ref, o_ref,
                 qbd_ref, s_ref, mx_ref, m_ref, l_ref, acc_ref, *, tile, nk, out_scale):
    hi = pl.program_id(1)
    qi = pl.program_id(2)
    qf = q_ref[...].astype(F32)
    lane = lax.broadcasted_iota(jnp.int32, qf.shape, 1)
    qbd_ref[0:tile, :] = jnp.where(lane < HEAD_DIM, qf, 0.0).astype(BF16)
    qbd_ref[tile:2 * tile, :] = jnp.where(lane >= HEAD_DIM, qf, 0.0).astype(BF16)
    m_ref[...] = jnp.full(m_ref.shape, NEG_BIG, F32)
    l_ref[...] = jnp.zeros(l_ref.shape, F32)
    acc_ref[...] = jnp.zeros(acc_ref.shape, F32)
    nt = (((1,), (1,)), ((), ()))
    tn = (((0,), (0,)), ((), ()))
    c_before = far_ref[hi, 0]
    c_after = far_ref[hi, 1]

    npairs = nk // 2 - 1
    near_lo = jnp.clip(qi // 2 - 1, 0, npairs)
    near_hi = jnp.clip(qi // 2 + 1, 0, npairs)

    def scores(kb, slot, with_bias):
        off = pl.multiple_of(kb * tile, tile)
        k = k_ref[pl.ds(off, tile), :]
        for c in range(2):
            cols = slice(c * tile, (c + 1) * tile)
            s = lax.dot_general(k, qbd_ref[cols, :], nt, preferred_element_type=F32)
            if with_bias:
                s = s + bias_ref[jnp.clip(kb - qi, -2, 2) + 2]
            s_ref[slot, :, cols] = s
            mx_ref[slot, :, cols] = jnp.max(s, axis=0, keepdims=True)

    def softmax_pv(kb, slot):
        off = pl.multiple_of(kb * tile, tile)
        had_bias = (kb == 0) | (kb == nk - 1) | ((kb >= 2 * near_lo + 1) & (kb <= 2 * near_hi))
        shift = jnp.where(had_bias, 0.0, jnp.where(kb < qi, c_before, c_after))
        v = v_ref[pl.ds(off, tile), :]
        for c in range(2):
            cols = slice(c * tile, (c + 1) * tile)
            m_old = m_ref[:, cols]
            m_new = jnp.maximum(m_old, mx_ref[slot, :, cols] + shift)
            alpha = jnp.exp2(m_old - m_new)
            p = jnp.exp2(s_ref[slot, :, cols] - (m_new - shift))
            l_ref[:, cols] = alpha * l_ref[:, cols] + jnp.sum(p, axis=0, keepdims=True)
            pv = lax.dot_general(v, p.astype(BF16), tn, preferred_element_type=F32)
            acc_ref[:, cols] = acc_ref[:, cols] * alpha + pv
            m_ref[:, cols] = m_new

    def pairs(lo, hi_j, with_bias):
        def body(jj, carry):
            kb = 2 * jj
            scores(kb + 1, 1, with_bias)
            softmax_pv(kb, 0)
            scores(kb + 2, 0, with_bias)
            softmax_pv(kb + 1, 1)
            return carry
        lax.fori_loop(lo, hi_j, body, 0)

    scores(0, 0, True)
    pairs(0, near_lo, False)
    pairs(near_lo, near_hi, True)
    pairs(near_hi, npairs, False)
    scores(nk - 1, 1, True)
    softmax_pv(nk - 2, 0)
    softmax_pv(nk - 1, 1)

    lam = lam_ref[0]
    inv = 1.0 / l_ref[...]
    acc = acc_ref[...]
    o_t = acc[:, :tile] * inv[:, :tile] - lam * (acc[:, tile:] * inv[:, tile:])
    ms = jnp.mean(o_t * o_t, axis=0, keepdims=True)
    o_t = o_t * lax.rsqrt(ms + LN_EPS)
    o_ref[...] = ((o_t.T * g_ref[...]) * out_scale).astype(BF16)


def _attention(qkv, bias_tab, far, lam, subln_g, b, s, d, lam_init):
    h = d // HEAD_W
    tile = ATT_TILE
    nq = s // tile
    assert nq >= 2 and nq % 2 == 0
    kern = functools.partial(_attn_kernel, tile=tile, nk=nq, out_scale=1.0 - lam_init)
    return pl.pallas_call(
        kern,
        out_shape=jax.ShapeDtypeStruct((b * s, d), BF16),
        grid=(b, h, nq),
        in_specs=[
            pl.BlockSpec(memory_space=pltpu.SMEM),
            pl.BlockSpec(memory_space=pltpu.SMEM),
            pl.BlockSpec((tile, HEAD_W), lambda bi, hi, qi: (bi * nq + qi, hi)),
            pl.BlockSpec((s, HEAD_W), lambda bi, hi, qi: (bi, h + hi)),
            pl.BlockSpec((s, HEAD_W), lambda bi, hi, qi: (bi, 2 * h + hi)),
            pl.BlockSpec((None, BIAS_TILES, tile, tile), lambda bi, hi, qi: (hi, 0, 0, 0)),
            pl.BlockSpec((1, HEAD_W), lambda bi, hi, qi: (0, 0)),
        ],
        out_specs=pl.BlockSpec((tile, HEAD_W), lambda bi, hi, qi: (bi * nq + qi, hi)),
        scratch_shapes=[pltpu.VMEM((2 * tile, HEAD_W), BF16),
                        pltpu.VMEM((2, tile, 2 * tile), F32),
                        pltpu.VMEM((2, 1, 2 * tile), F32),
                        pltpu.VMEM((1, 2 * tile), F32),
                        pltpu.VMEM((1, 2 * tile), F32),
                        pltpu.VMEM((HEAD_W, 2 * tile), F32)],
        compiler_params=_cparams("parallel", "parallel", "arbitrary"),
        name="diff_attention",
    )(lam, far, qkv, qkv, qkv, bias_tab, subln_g)


def _rel_bucket(rel):
    nb = REL_BUCKETS // 2
    max_exact = nb // 2
    ret = jnp.where(rel > 0, nb, 0)
    n = jnp.abs(rel)
    nf = jnp.maximum(n, 1).astype(F32)
    large = max_exact + (jnp.log(nf / max_exact) / math.log(REL_MAX_DIST / max_exact)
                         * (nb - max_exact)).astype(jnp.int32)
    large = jnp.minimum(large, nb - 1)
    return ret + jnp.where(n < max_exact, n, large)


def _bias_kernel(rb_ref, bucket_ref, o_ref, *, tile):
    h = pl.program_id(0)
    half = REL_BUCKETS // 2
    ranges = ((half - 1, half), (0, half), (0, REL_BUCKETS), (half, REL_BUCKETS), (REL_BUCKETS - 1, REL_BUCKETS))
    rows = 4 * SUBLANES
    for t in range(BIAS_TILES):
        def body(i, carry, t=t):
            r0 = pl.multiple_of(i * rows, rows)
            bk = bucket_ref[t, pl.ds(r0, rows), :]
            acc = jnp.zeros(bk.shape, F32)
            for bkt in range(*ranges[t]):
                acc = jnp.where(bk == bkt, rb_ref[h, bkt], acc)
            o_ref[t, pl.ds(r0, rows), :] = acc
            return carry
        lax.fori_loop(0, tile // rows, body, 0)


def _bias_tables(rel_bias, tile):
    assert tile >= REL_MAX_DIST
    nh = rel_bias.shape[1]
    dd = jnp.arange(-2, 3, dtype=jnp.int32)[:, None, None]
    kk = jnp.arange(tile, dtype=jnp.int32)[None, :, None]
    qq = jnp.arange(tile, dtype=jnp.int32)[None, None, :]
    bucket = _rel_bucket(dd * tile + kk - qq)
    rb = jnp.transpose(rel_bias.astype(F32)) * LOG2E
    far = jnp.stack([rb[:, REL_BUCKETS // 2 - 1], rb[:, REL_BUCKETS - 1]], axis=1)
    tab = pl.pallas_call(
        functools.partial(_bias_kernel, tile=tile),
        out_shape=jax.ShapeDtypeStruct((nh, BIAS_TILES, tile, tile), F32),
        grid=(nh,),
        in_specs=[pl.BlockSpec(memory_space=pltpu.SMEM),
                  pl.BlockSpec((BIAS_TILES, tile, tile), lambda i: (0, 0, 0))],
        out_specs=pl.BlockSpec((None, BIAS_TILES, tile, tile), lambda i: (i, 0, 0, 0)),
        compiler_params=_cparams("parallel"),
        name="rel_bias_tiles",
    )(rb, bucket)
    return tab, far


def _ln(z, g, b):
    mu = jnp.mean(z, axis=-1, keepdims=True)
    zc = z - mu
    var = jnp.mean(zc * zc, axis=-1, keepdims=True)
    return zc * lax.rsqrt(var + LN_EPS) * g + b


def _route_epilogue(xn, wr_hi_ref, wr_lo_ref, br_ref, route_ref, counts_ref, base_ref):
    rows = xn.shape[0]
    x_hi, x_lo = _split_bf16(xn)
    nt = (((1,), (1,)), ((), ()))
    logits = (lax.dot_general(wr_hi_ref[...], x_hi, nt, preferred_element_type=F32)
              + lax.dot_general(wr_hi_ref[...], x_lo, nt, preferred_element_type=F32)
              + lax.dot_general(wr_lo_ref[...], x_hi, nt, preferred_element_type=F32))
    logits = logits + br_ref[...]

    lg = logits[0:FINE_ROW0]
    gmax = jnp.max(lg, axis=0, keepdims=True)
    pg_top = 1.0 / jnp.sum(jnp.exp(lg - gmax), axis=0, keepdims=True)
    gio = lax.broadcasted_iota(jnp.int32, lg.shape, 0).astype(F32)
    g_idx = jnp.min(jnp.where(lg == gmax, gio, float(FINE_ROW0)), axis=0, keepdims=True)

    sel = jnp.zeros((EXPERTS_PER_GROUP, rows), F32)
    for g in range(N_GROUPS):
        r0 = FINE_ROW0 + g * EXPERTS_PER_GROUP
        sel = jnp.where(g_idx == float(g), logits[r0:r0 + EXPERTS_PER_GROUP], sel)
    eio = lax.broadcasted_iota(jnp.int32, sel.shape, 0).astype(F32)
    v1 = jnp.max(sel, axis=0, keepdims=True)
    i1 = jnp.min(jnp.where(sel == v1, eio, float(EXPERTS_PER_GROUP)), axis=0, keepdims=True)
    rest = jnp.where(eio == i1, -jnp.inf, sel)
    v2 = jnp.max(rest, axis=0, keepdims=True)
    i2 = jnp.min(jnp.where(rest == v2, eio, float(EXPERTS_PER_GROUP)), axis=0, keepdims=True)
    e2 = jnp.exp(v2 - v1)
    den = 1.0 + e2
    gate1 = pg_top / den
    gate2 = pg_top * e2 / den
    eid1 = g_idx * float(EXPERTS_PER_GROUP) + i1
    eid2 = g_idx * float(EXPERTS_PER_GROUP) + i2

    xio = lax.broadcasted_iota(jnp.int32, (N_EXPERTS, rows), 0).astype(F32)
    cnt = jnp.where((xio == eid1) | (xio == eid2), 1.0, 0.0)
    base_new = base_ref[...] + jnp.sum(cnt, axis=1, keepdims=True)
    base_ref[...] = base_new
    counts_ref[...] = jnp.broadcast_to(base_new, counts_ref.shape)
    rio = lax.broadcasted_iota(jnp.int32, route_ref.shape, 0)
    out = jnp.zeros(route_ref.shape, F32)
    for r, val in enumerate((eid1, eid2, gate1, gate2)):
        out = jnp.where(rio == r, val, out)
    route_ref[...] = out


def _store_rows(xn, xn_ref, xt_ref):
    xn_ref[...] = xn
    xt_ref[...] = _tiles_from_rows(xn)


def _proj_ln_kernel(a_ref, w_ref, x_ref, g_ref, b_ref, wr_hi_ref, wr_lo_ref, br_ref,
                    xn_ref, xt_ref, route_ref, counts_ref, base_ref):
    @pl.when(pl.program_id(0) == 0)
    def _():
        base_ref[...] = jnp.zeros(base_ref.shape, F32)

    h = jnp.dot(a_ref[...], w_ref[...], preferred_element_type=F32)
    xn = _ln(ALPHA * x_ref[...] + h, g_ref[...], b_ref[...])
    _store_rows(xn, xn_ref, xt_ref)
    _route_epilogue(xn, wr_hi_ref, wr_lo_ref, br_ref, route_ref, counts_ref, base_ref)


def _fft2_ln_kernel(a_ref, g_tab_ref, mr_ref, mi_ref, bo_ref, x_ref, g_ref, b_ref,
                    wr_hi_ref, wr_lo_ref, br_ref,
                    xn_ref, xt_ref, route_ref, counts_ref, base_ref):
    @pl.when(pl.program_id(0) == 0)
    def _():
        base_ref[...] = jnp.zeros(base_ref.shape, F32)

    rows = xn_ref.shape[0]
    d = xn_ref.shape[1]
    kb = g_tab_ref.shape[0]
    s1 = rows // kb
    at = pltpu.einshape("njd->jnd", a_ref[...])
    vr, vi = [], []
    for j in range(kb):
        aj = jnp.concatenate([at[j], at[kb + j]], axis=0)
        vj = jnp.dot(g_tab_ref[j], aj, preferred_element_type=F32)
        vr.append(vj[:s1])
        vi.append(vj[s1:])
    vr = pltpu.einshape("jkd->kjd", jnp.stack(vr)).reshape(rows, d)
    vi = pltpu.einshape("jkd->kjd", jnp.stack(vi)).reshape(rows, d)
    h = (jnp.dot(vr.astype(BF16), mr_ref[...], preferred_element_type=F32)
         + jnp.dot(vi.astype(BF16), mi_ref[...], preferred_element_type=F32)
         + bo_ref[...])
    xn = _ln(ALPHA * x_ref[...].reshape(rows, d) + h, g_ref[...], b_ref[...])
    _store_rows(xn, xn_ref, xt_ref)
    _route_epilogue(xn, wr_hi_ref, wr_lo_ref, br_ref, route_ref, counts_ref, base_ref)


def _router_specs(d, rows):
    const2 = lambda i: (0, 0)
    ins = [pl.BlockSpec((1, d), const2), pl.BlockSpec((1, d), const2),
           pl.BlockSpec((ROUTER_ROWS, d), const2), pl.BlockSpec((ROUTER_ROWS, d), const2),
           pl.BlockSpec((ROUTER_ROWS, 1), const2)]
    outs = [pl.BlockSpec((rows, d), lambda i: (i, 0)),
            pl.BlockSpec((rows, d // LANES, LANES), lambda i: (i, 0, 0)),
            pl.BlockSpec((SUBLANES, rows), lambda i: (0, i)),
            pl.BlockSpec((N_EXPERTS, LANES), const2)]
    return ins, outs


def _router_out_shapes(t, d):
    return (jax.ShapeDtypeStruct((t, d), F32),
            jax.ShapeDtypeStruct((t, d // LANES, LANES), F32),
            jax.ShapeDtypeStruct((SUBLANES, t), F32),
            jax.ShapeDtypeStruct((N_EXPERTS, LANES), F32))


def _proj_ln_router(att, wo_bf16, x2, ln_g, ln_b, router):
    t, d = x2.shape
    rows = ROW_TILE
    tail_in, outs = _router_specs(d, rows)
    return pl.pallas_call(
        _proj_ln_kernel,
        out_shape=_router_out_shapes(t, d),
        grid=(t // rows,),
        in_specs=[pl.BlockSpec((rows, d), lambda i: (i, 0)),
                  pl.BlockSpec((d, d), lambda i: (0, 0)),
                  pl.BlockSpec((rows, d), lambda i: (i, 0))] + tail_in,
        out_specs=outs,
        scratch_shapes=[pltpu.VMEM((N_EXPERTS, 1), F32)],
        compiler_params=_cparams("arbitrary"),
        name="proj_ln_router",
    )(att, wo_bf16, x2, ln_g, ln_b, *router)


def _fft1_kernel(f_ref, x_ref, o_ref):
    xt = pltpu.einshape("njd->jnd", x_ref[...])
    for j in range(xt.shape[0]):
        o_ref[j] = jnp.dot(f_ref[...], xt[j].astype(BF16), preferred_element_type=F32).astype(BF16)


def _fft_stage1(x3):
    b, s, d = x3.shape
    s1 = FFT_S1
    s2 = s // s1
    kb = FFT_KB
    k = np.arange(s2)
    ang = 2.0 * np.pi * ((k[:, None] * k[None, :]) % s2) / s2
    f = np.stack([np.cos(ang), -np.sin(ang)])
    f = f.reshape(2, s2 // kb, kb, s2).transpose(1, 0, 2, 3).reshape(2 * s2, s2)
    f = jnp.asarray(f, dtype=BF16)
    n1_tile = SUBLANES
    return pl.pallas_call(
        _fft1_kernel,
        out_shape=jax.ShapeDtypeStruct((b, s1, 2 * s2, d), BF16),
        grid=(b, s1 // n1_tile),
        in_specs=[pl.BlockSpec((2 * s2, s2), lambda bi, ni: (0, 0)),
                  pl.BlockSpec((None, s2, n1_tile, d), lambda bi, ni: (bi, 0, ni, 0))],
        out_specs=pl.BlockSpec((None, n1_tile, 2 * s2, d), lambda bi, ni: (bi, ni, 0, 0)),
        compiler_params=_cparams("parallel", "parallel"),
        name="fft_stage1",
    )(f, x3.reshape(b, s2, s1, d))


def _fft_stage2_table(s):
    s1 = FFT_S1
    s2 = s // s1
    k2 = np.arange(s2)[:, None, None]
    k1 = np.arange(s1)[None, :, None]
    n1 = np.arange(s1)[None, None, :]
    ang = 2.0 * np.pi * ((k1 * n1 * s2 + k2 * n1) % s) / s
    gr, gi = np.cos(ang), -np.sin(ang)
    g = np.concatenate([np.concatenate([gr, -gi], axis=2), np.concatenate([gi, gr], axis=2)], axis=1)
    return jnp.asarray(g, dtype=BF16)


def _fold_kernel(cs_hi_ref, cs_lo_ref, w_ref, o_ref):
    w_hi, w_lo = _split_bf16(w_ref[...])
    o_ref[...] = (jnp.dot(cs_hi_ref[...], w_hi, preferred_element_type=F32)
                  + jnp.dot(cs_hi_ref[...], w_lo, preferred_element_type=F32)
                  + jnp.dot(cs_lo_ref[...], w_hi, preferred_element_type=F32)).astype(BF16)


def _fold_channel_dft(w_o, s):
    d = w_o.shape[0]
    cg = d // FNET_GROUPS
    c = np.arange(cg)
    ang = 2.0 * np.pi * ((c[:, None] * c[None, :]) % cg) / cg
    scale = 1.0 / math.sqrt(s * cg)
    cs = jnp.asarray(np.stack([np.cos(ang), np.sin(ang)]) * scale, dtype=F32)
    cs_hi, cs_lo = _split_bf16(cs)
    out = pl.pallas_call(
        _fold_kernel,
        out_shape=jax.ShapeDtypeStruct((2, d, d), BF16),
        grid=(2, FNET_GROUPS),
        in_specs=[pl.BlockSpec((None, cg, cg), lambda ci, gi: (ci, 0, 0)),
                  pl.BlockSpec((None, cg, cg), lambda ci, gi: (ci, 0, 0)),
                  pl.BlockSpec((cg, d), lambda ci, gi: (gi, 0))],
        out_specs=pl.BlockSpec((None, cg, d), lambda ci, gi: (ci, gi, 0)),
        compiler_params=_cparams("parallel", "parallel"),
        name="fold_channel_dft",
    )(cs_hi, cs_lo, w_o.astype(F32))
    return out[0], out[1]


def _fft2_ln_router(a1, gbig, mr, mi, b_o, x3, ln_g, ln_b, router):
    b, s, d = x3.shape
    s1 = FFT_S1
    s2 = s // s1
    kb = FFT_KB
    nblk = s2 // kb
    rows = s1 * kb
    t = b * s
    a5 = a1.reshape(b, s1, nblk, 2 * kb, d)
    x5 = x3.reshape(b, s1, nblk, kb, d)
    tail_in, outs = _router_specs(d, rows)
    return pl.pallas_call(
        _fft2_ln_kernel,
        out_shape=_router_out_shapes(t, d),
        grid=(nblk * b,),
        in_specs=[pl.BlockSpec((None, s1, None, 2 * kb, d), lambda i: (i % b, 0, i // b, 0, 0)),
                  pl.BlockSpec((kb, 2 * s1, 2 * s1), lambda i: (i // b, 0, 0)),
                  pl.BlockSpec((d, d), lambda i: (0, 0)),
                  pl.BlockSpec((d, d), lambda i: (0, 0)),
                  pl.BlockSpec((1, d), lambda i: (0, 0)),
                  pl.BlockSpec((None, s1, None, kb, d), lambda i: (i % b, 0, i // b, 0, 0))] + tail_in,
        out_specs=outs,
        scratch_shapes=[pltpu.VMEM((N_EXPERTS, 1), F32)],
        compiler_params=_cparams("arbitrary"),
        name="fft2_ln_router",
    )(a5, gbig, mr, mi, b_o, x5, ln_g, ln_b, *router)


def _rows_from_tiles(ref, *lead):
    xt = pltpu.einshape("rcl->crl", ref[lead] if lead else ref[...])
    return jnp.concatenate([xt[c] for c in range(xt.shape[0])], axis=1)


def _tiles_from_rows(y):
    yt = jnp.stack([y[:, c * LANES:(c + 1) * LANES] for c in range(y.shape[1] // LANES)])
    return pltpu.einshape("crl->rcl", yt)


def _expert_kernel(bexp_ref, nused_ref, src_ref, src_next_ref, dst_ref, xn_ref, w1_ref, w3_ref, w2_ref, out_ref,
                   xbuf, ybuf, w13_bf, w2_bf, gsem, ssem, *, de):
    i = pl.program_id(0)
    last = pl.num_programs(0) - 1
    n_used = nused_ref[0]
    slot = lax.rem(i, 2)
    blk = xbuf.shape[1]
    half = blk // 2

    def gather(idx_ref, sl):
        def body(r, carry):
            for pr in range(2):
                row = pr * half + r
                pltpu.make_async_copy(xn_ref.at[idx_ref[0, 0, row]], xbuf.at[sl, row],
                                      gsem.at[sl]).start(priority=pr)
            return carry
        lax.fori_loop(0, half, body, 0, unroll=8)

    def gather_wait(sl):
        pltpu.make_async_copy(xn_ref.at[pl.ds(0, blk)], xbuf.at[sl], gsem.at[sl]).wait()

    def scatter_wait(sl):
        pltpu.make_async_copy(ybuf.at[sl], out_ref.at[pl.ds(0, blk)], ssem.at[sl]).wait()

    @pl.when(i == 0)
    def _():
        gather(src_ref, 0)

    @pl.when(i + 1 < n_used)
    def _():
        gather(src_next_ref, 1 - slot)

    @pl.when((i == 0) | (bexp_ref[i] != bexp_ref[jnp.maximum(i - 1, 0)]))
    def _():
        w13_bf[:, :de] = w1_ref[...].astype(BF16)
        w13_bf[:, de:] = w3_ref[...].astype(BF16)
        w2_bf[...] = w2_ref[...].astype(BF16)

    @pl.when(i >= 2)
    def _():
        scatter_wait(slot)

    @pl.when(i < n_used)
    def _():
        gather_wait(slot)
        xb = _rows_from_tiles(xbuf, slot).astype(BF16)
        h = jnp.dot(xb, w13_bf[...], preferred_element_type=F32)
        a = h[:, :de]
        act = (a / (1.0 + jnp.exp(-a))) * h[:, de:]
        y = jnp.dot(act.astype(BF16), w2_bf[...], preferred_element_type=F32)
        ybuf[slot] = _tiles_from_rows(y)

        def body(r, carry):
            for pr in range(2):
                row = pr * half + r
                pltpu.make_async_copy(ybuf.at[slot, row], out_ref.at[dst_ref[0, 0, row]],
                                      ssem.at[slot]).start(priority=pr)
            return carry
        lax.fori_loop(0, half, body, 0, unroll=8)

    @pl.when(i >= n_used)
    def _():
        ybuf[slot] = jnp.zeros(ybuf.shape[1:], F32)
        pltpu.make_async_copy(ybuf.at[slot], out_ref.at[pl.ds(dst_ref[0, 0, 0], blk)], ssem.at[slot]).start()

    @pl.when(i == last)
    def _():
        @pl.when(i >= 1)
        def _():
            scatter_wait(1 - slot)
        scatter_wait(slot)


def _expert_mlp(xn_tiles, src_tok, dst_row, layer, w1, w3, w2, block_exp, n_used):
    nb, _, blk = src_tok.shape
    tile_rows = xn_tiles.shape[1]
    d = tile_rows * LANES
    de = w2.shape[2]

    def w_map(i, bexp, nused):
        return (layer, bexp[i], 0, 0)

    smem_block = functools.partial(pl.BlockSpec, (1, 1, blk), memory_space=pltpu.SMEM)
    grid_spec = pltpu.PrefetchScalarGridSpec(
        num_scalar_prefetch=2,
        grid=(nb,),
        in_specs=[smem_block(lambda i, bexp, nused: (i, 0, 0)),
                  smem_block(lambda i, bexp, nused: (jnp.minimum(i + 1, nb - 1), 0, 0)),
                  smem_block(lambda i, bexp, nused: (i, 0, 0)),
                  pl.BlockSpec(memory_space=pl.ANY),
                  pl.BlockSpec((None, None, d, de), w_map),
                  pl.BlockSpec((None, None, d, de), w_map),
                  pl.BlockSpec((None, None, de, d), w_map)],
        out_specs=pl.BlockSpec(memory_space=pl.ANY),
        scratch_shapes=[pltpu.VMEM((2, blk, tile_rows, LANES), F32), pltpu.VMEM((2, blk, tile_rows, LANES), F32),
                        pltpu.VMEM((d, 2 * de), BF16), pltpu.VMEM((de, d), BF16),
                        pltpu.SemaphoreType.DMA((2,)), pltpu.SemaphoreType.DMA((2,))],
    )
    return pl.pallas_call(
        functools.partial(_expert_kernel, de=de),
        out_shape=jax.ShapeDtypeStruct((nb * blk, tile_rows, LANES), F32),
        grid_spec=grid_spec,
        compiler_params=_cparams("arbitrary"),
        name="expert_mlp",
    )(block_exp, n_used, src_tok, src_tok, dst_row, xn_tiles, w1, w3, w2)


def _combine_ln_kernel(y1_ref, y2_ref, x_ref, gates_ref, g_ref, b_ref, o_ref):
    gates = gates_ref[...]
    m = gates[:, 0:1] * _rows_from_tiles(y1_ref) + gates[:, 1:2] * _rows_from_tiles(y2_ref)
    y = _ln(ALPHA * x_ref[...] + m, g_ref[...], b_ref[...])
    o_ref[...] = y.reshape(o_ref.shape)


def _combine_ln(ys, xn, gates, ln_g, ln_b, out_shape, out_spec):
    t, d = xn.shape
    rows = ROW_TILE
    nsteps = t // rows
    tile_rows = ys.shape[1]
    return pl.pallas_call(
        _combine_ln_kernel,
        out_shape=out_shape,
        grid=(nsteps,),
        in_specs=[pl.BlockSpec((rows, tile_rows, LANES), lambda i: (i, 0, 0)),
                  pl.BlockSpec((rows, tile_rows, LANES), lambda i: (nsteps + i, 0, 0)),
                  pl.BlockSpec((rows, d), lambda i: (i, 0)),
                  pl.BlockSpec((rows, 2), lambda i: (i, 0)),
                  pl.BlockSpec((1, d), lambda i: (0, 0)),
                  pl.BlockSpec((1, d), lambda i: (0, 0))],
        out_specs=out_spec,
        compiler_params=_cparams("parallel"),
        name="moe_combine_ln",
    )(ys, ys, xn, gates, ln_g, ln_b)


def _moe_ln(xn, xn_tiles, route, counts, experts_w, ln_g, ln_b, out_shape, out_spec):
    t, d = xn.shape
    blk = EXPERT_BLOCK
    nb = (2 * t) // blk + N_EXPERTS
    p_rows = nb * blk
    n_dummy = p_rows - 2 * t
    eid = route[0:2].astype(jnp.int32).reshape(2 * t)
    cnt = counts[:, 0].astype(jnp.int32)
    padded = (cnt + blk - 1) // blk * blk
    pends = jnp.cumsum(padded)
    pad_ends = jnp.cumsum(padded - cnt)
    dummy = jnp.arange(n_dummy, dtype=jnp.int32) - (p_rows - pends[-1])
    dummy_expert = jnp.where(dummy < 0, N_EXPERTS,
                             jnp.sum((pad_ends[None, :] <= dummy[:, None]).astype(jnp.int32), axis=1))
    group = jnp.concatenate([2 * eid, 2 * dummy_expert + 1])
    slot_asg = lax.sort(group * p_rows + jnp.arange(p_rows, dtype=jnp.int32)) % p_rows
    src_tok = jnp.where(slot_asg < t, slot_asg, jnp.where(slot_asg < 2 * t, slot_asg - t, 0))
    block_row0 = jnp.arange(nb, dtype=jnp.int32)[:, None] * blk
    block_exp = jnp.minimum(jnp.sum((pends[None, :] <= block_row0).astype(jnp.int32), axis=1), N_EXPERTS - 1)
    n_used = (pends[-1:] // blk).astype(jnp.int32)
    gates = jnp.transpose(route[2:4])

    ys = _expert_mlp(xn_tiles, src_tok.reshape(nb, 1, blk), slot_asg.reshape(nb, 1, blk), *experts_w,
                     block_exp, n_used)
    return _combine_ln(ys, xn, gates, ln_g, ln_b, out_shape, out_spec)


def _router_params(w_group, b_group, w_fine, b_fine, rows):
    d = w_group.shape[0]
    w = jnp.zeros((ROUTER_ROWS, d), F32)
    w = w.at[0:N_GROUPS].set(jnp.transpose(w_group).astype(F32))
    w = w.at[FINE_ROW0:FINE_ROW0 + N_EXPERTS].set(jnp.transpose(w_fine).astype(F32))
    bias = jnp.zeros((ROUTER_ROWS, 1), F32).at[N_GROUPS:FINE_ROW0, 0].set(NEG_BIG)
    bias = bias.at[0:N_GROUPS, 0].set(b_group.astype(F32))
    bias = bias.at[FINE_ROW0:FINE_ROW0 + N_EXPERTS, 0].set(b_fine.astype(F32))
    w_hi, w_lo = _split_bf16(w)
    return w_hi, w_lo, bias


def _row(v):
    return v.reshape(1, -1).astype(F32)


def _trunk(x, p, shared):
    b, s, d = x.shape
    t = b * s
    h = d // HEAD_W
    assert s % ATT_TILE == 0 and t % ROW_TILE == 0 and s % (FFT_S1 * FFT_KB) == 0
    assert ROW_TILE == FFT_S1 * FFT_KB
    flat_shape = jax.ShapeDtypeStruct((t, d), F32)
    flat_spec = pl.BlockSpec((ROW_TILE, d), lambda i: (i, 0))
    x2 = x.reshape(t, d)
    for i in range(DEPTH):
        j = i // 2
        if i % 2 == 0:
            qkv = _qkv_proj(x2, shared["wqkv"][j])
            att = _attention(qkv, *shared["bias"], shared["lam"][j], shared["subln_g"][j], b, s, d,
                             shared["lam_init"][j])
            xn, xt, route, counts = _proj_ln_router(att, shared["wo"][j], x2, _row(p["ln1_g"][i]),
                                                    _row(p["ln1_b"][i]), shared["router"][i])
            x2 = _moe_ln(xn, xt, route, counts, shared["experts"][i],
                         _row(p["ln2_g"][i]), _row(p["ln2_b"][i]), flat_shape, flat_spec)
        else:
            x3 = x2.reshape(b, s, d)
            a1 = _fft_stage1(x3)
            mr, mi = shared["fold"][(j, s)]
            xn, xt, route, counts = _fft2_ln_router(a1, shared["gbig"][s], mr, mi, _row(p["fnet_b_o"][j]), x3,
                                                    _row(p["ln1_g"][i]), _row(p["ln1_b"][i]),
                                                    shared["router"][i])
            nblk = s // (FFT_S1 * FFT_KB)
            out_shape = jax.ShapeDtypeStruct((b, FFT_S1, nblk, FFT_KB, d), F32)
            out_spec = pl.BlockSpec((None, FFT_S1, None, FFT_KB, d), lambda i2: (i2 % b, 0, i2 // b, 0, 0))
            y5 = _moe_ln(xn, xt, route, counts, shared["experts"][i],
                         _row(p["ln2_g"][i]), _row(p["ln2_b"][i]), out_shape, out_spec)
            x2 = y5.reshape(t, d)
    return x2.reshape(b, s, d)


def kernel(x_prompt, x_sample, rel_bias, attn_w_qkv, attn_lambda_q1, attn_lambda_k1, attn_lambda_q2, attn_lambda_k2, attn_subln_g, attn_w_o, fnet_w_o, fnet_b_o, ln1_g, ln1_b, ln2_g, ln2_b, moe_w_group, moe_b_group, moe_w_fine, moe_b_fine, moe_w1, moe_w3, moe_w2):
    d = x_prompt.shape[-1]
    p = dict(fnet_b_o=fnet_b_o, ln1_g=ln1_g, ln1_b=ln1_b, ln2_g=ln2_g, ln2_b=ln2_b)
    n_attn = attn_w_qkv.shape[0]
    n_fnet = fnet_w_o.shape[0]
    seqs = sorted({x_prompt.shape[1], x_sample.shape[1]})

    qscale = jnp.concatenate([jnp.full((d,), HEAD_DIM ** -0.5 * LOG2E, F32), jnp.ones((2 * d,), F32)])
    shared = dict(
        wqkv=[(attn_w_qkv[j].astype(F32) * qscale).astype(BF16) for j in range(n_attn)],
        wo=[attn_w_o[j].astype(BF16) for j in range(n_attn)],
        subln_g=[_row(attn_subln_g[j]) for j in range(n_attn)],
        lam_init=[0.8 - 0.6 * math.exp(-0.3 * (2 * j)) for j in range(n_attn)],
        bias=_bias_tables(rel_bias, ATT_TILE),
        router=[_router_params(moe_w_group[i], moe_b_group[i], moe_w_fine[i], moe_b_fine[i], ROW_TILE)
                for i in range(DEPTH)],
        experts=[(i, moe_w1, moe_w3, moe_w2) for i in range(DEPTH)],
        gbig={s: _fft_stage2_table(s) for s in seqs},
        fold={(j, s): _fold_channel_dft(fnet_w_o[j], s) for j in range(n_fnet) for s in seqs},
    )
    shared["lam"] = [
        (jnp.exp(jnp.sum(attn_lambda_q1[j].astype(F32) * attn_lambda_k1[j].astype(F32)))
         - jnp.exp(jnp.sum(attn_lambda_q2[j].astype(F32) * attn_lambda_k2[j].astype(F32)))
         + shared["lam_init"][j]).reshape(1).astype(F32)
        for j in range(n_attn)]

    return (_trunk(x_prompt, p, shared), _trunk(x_sample, p, shared))
```

```python
import functools
import math

import numpy as np
import jax
import jax.numpy as jnp
from jax import lax
from jax.experimental import pallas as pl
from jax.experimental.pallas import tpu as pltpu

F32 = jnp.float32
BF16 = jnp.bfloat16

HEAD_DIM = 64
HEAD_W = 2 * HEAD_DIM
REL_BUCKETS = 32
REL_MAX_DIST = 128
FNET_GROUPS = 4
N_GROUPS = 4
EXPERTS_PER_GROUP = 8
N_EXPERTS = N_GROUPS * EXPERTS_PER_GROUP
DEPTH = 2
ALPHA = (2 * DEPTH) ** 0.25
LN_EPS = 1e-5
LOG2E = 1.4426950408889634

LANES = 128
SUBLANES = 8
VMEM_LIMIT_BYTES = 52 * 1024 * 1024

ROW_TILE = 512
ATT_TILE = 512
BIAS_TILES = 5
EXPERT_BLOCK = 512
FFT_S1 = 64
FFT_KB = SUBLANES
ROUTER_ROWS = 48
FINE_ROW0 = 8
NEG_BIG = -1e30


def _cparams(*sem):
    return pltpu.CompilerParams(dimension_semantics=sem, vmem_limit_bytes=VMEM_LIMIT_BYTES)


def _split_bf16(a):
    hi = a.astype(BF16)
    lo = (a - hi.astype(F32)).astype(BF16)
    return hi, lo


def _qkv_kernel(x_ref, w_ref, o_ref, *, d):
    xb = x_ref[...].astype(BF16)
    for c in range(w_ref.shape[1] // d):
        o_ref[:, c * d:(c + 1) * d] = jnp.dot(
            xb, w_ref[:, c * d:(c + 1) * d], preferred_element_type=F32).astype(BF16)


def _qkv_proj(x2, w_bf16):
    t, d = x2.shape
    n = w_bf16.shape[1]
    return pl.pallas_call(
        functools.partial(_qkv_kernel, d=d),
        out_shape=jax.ShapeDtypeStruct((t, n), BF16),
        grid=(t // ROW_TILE,),
        in_specs=[pl.BlockSpec((ROW_TILE, d), lambda i: (i, 0)),
                  pl.BlockSpec((d, n), lambda i: (0, 0))],
        out_specs=pl.BlockSpec((ROW_TILE, n), lambda i: (i, 0)),
        compiler_params=_cparams("parallel"),
        name="qkv_proj",
    )(x2, w_bf16)


def _attn_kernel(lam_ref, far_ref, q_ref, k_ref, v_ref, bias_ref, g_ref, o_ref,
                 qbd_ref, s_ref, mx_ref, m_ref, l_ref, acc_ref, *, tile, nk, out_scale):
    hi = pl.program_id(1)
    qi = pl.program_id(2)
    qf = q_ref[...].astype(F32)
    lane = lax.broadcasted_iota(jnp.int32, qf.shape, 1)
    qbd_ref[0:tile, :] = jnp.where(lane < HEAD_DIM, qf, 0.0).astype(BF16)
    qbd_ref[tile:2 * tile, :] = jnp.where(lane >= HEAD_DIM, qf, 0.0).astype(BF16)
    m_ref[...] = jnp.full(m_ref.shape, NEG_BIG, F32)
    l_ref[...] = jnp.zeros(l_ref.shape, F32)
    acc_ref[...] = jnp.zeros(acc_ref.shape, F32)
    nt = (((1,), (1,)), ((), ()))
    tn = (((0,), (0,)), ((), ()))
    c_before = far_ref[hi, 0]
    c_after = far_ref[hi, 1]

    npairs = nk // 2 - 1
    near_lo = jnp.clip(qi // 2 - 1, 0, npairs)
    near_hi = jnp.clip(qi // 2 + 1, 0, npairs)

    def scores(kb, slot, with_bias):
        off = pl.multiple_of(kb * tile, tile)
        k = k_ref[pl.ds(off, tile), :]
        for c in range(2):
            cols = slice(c * tile, (c + 1) * tile)
            s = lax.dot_general(k, qbd_ref[cols, :], nt, preferred_element_type=F32)
            if with_bias:
                s = s + bias_ref[jnp.clip(kb - qi, -2, 2) + 2]
            s_ref[slot, :, cols] = s
            mx_ref[slot, :, cols] = jnp.max(s, axis=0, keepdims=True)

    def softmax_pv(kb, slot):
        off = pl.multiple_of(kb * tile, tile)
        had_bias = (kb == 0) | (kb == nk - 1) | ((kb >= 2 * near_lo + 1) & (kb <= 2 * near_hi))
        shift = jnp.where(had_bias, 0.0, jnp.where(kb < qi, c_before, c_after))
        v = v_ref[pl.ds(off, tile), :]
        for c in range(2):
            cols = slice(c * tile, (c + 1) * tile)
            m_old = m_ref[:, cols]
            m_new = jnp.maximum(m_old, mx_ref[slot, :, cols] + shift)
            alpha = jnp.exp2(m_old - m_new)
            p = jnp.exp2(s_ref[slot, :, cols] - (m_new - shift))
            l_ref[:, cols] = alpha * l_ref[:, cols] + jnp.sum(p, axis=0, keepdims=True)
            pv = lax.dot_general(v, p.astype(BF16), tn, preferred_element_type=F32)
            acc_ref[:, cols] = acc_ref[:, cols] * alpha + pv
            m_ref[:, cols] = m_new

    def pairs(lo, hi_j, with_bias):
        def body(jj, carry):
            kb = 2 * jj
            scores(kb + 1, 1, with_bias)
            softmax_pv(kb, 0)
            scores(kb + 2, 0, with_bias)
            softmax_pv(kb + 1, 1)
            return carry
        lax.fori_loop(lo, hi_j, body, 0)

    scores(0, 0, True)
    pairs(0, near_lo, False)
    pairs(near_lo, near_hi, True)
    pairs(near_hi, npairs, False)
    scores(nk - 1, 1, True)
    softmax_pv(nk - 2, 0)
    softmax_pv(nk - 1, 1)

    lam = lam_ref[0]
    inv = 1.0 / l_ref[...]
    acc = acc_ref[...]
    o_t = acc[:, :tile] * inv[:, :tile] - lam * (acc[:, tile:] * inv[:, tile:])
    ms = jnp.mean(o_t * o_t, axis=0, keepdims=True)
    o_t = o_t * lax.rsqrt(ms + LN_EPS)
    o_ref[...] = ((o_t.T * g_ref[...]) * out_scale).astype(BF16)


def _attention(qkv, bias_tab, far, lam, subln_g, b, s, d, lam_init):
    h = d // HEAD_W
    tile = ATT_TILE
    nq = s // tile
    assert nq >= 2 and nq % 2 == 0
    kern = functools.partial(_attn_kernel, tile=tile, nk=nq, out_scale=1.0 - lam_init)
    return pl.pallas_call(
        kern,
        out_shape=jax.ShapeDtypeStruct((b * s, d), BF16),
        grid=(b, h, nq),
        in_specs=[
            pl.BlockSpec(memory_space=pltpu.SMEM),
            pl.BlockSpec(memory_space=pltpu.SMEM),
            pl.BlockSpec((tile, HEAD_W), lambda bi, hi, qi: (bi * nq + qi, hi)),
            pl.BlockSpec((s, HEAD_W), lambda bi, hi, qi: (bi, h + hi)),
            pl.BlockSpec((s, HEAD_W), lambda bi, hi, qi: (bi, 2 * h + hi)),
            pl.BlockSpec((None, BIAS_TILES, tile, tile), lambda bi, hi, qi: (hi, 0, 0, 0)),
            pl.BlockSpec((1, HEAD_W), lambda bi, hi, qi: (0, 0)),
        ],
        out_specs=pl.BlockSpec((tile, HEAD_W), lambda bi, hi, qi: (bi * nq + qi, hi)),
        scratch_shapes=[pltpu.VMEM((2 * tile, HEAD_W), BF16),
                        pltpu.VMEM((2, tile, 2 * tile), F32),
                        pltpu.VMEM((2, 1, 2 * tile), F32),
                        pltpu.VMEM((1, 2 * tile), F32),
                        pltpu.VMEM((1, 2 * tile), F32),
                        pltpu.VMEM((HEAD_W, 2 * tile), F32)],
        compiler_params=_cparams("parallel", "parallel", "arbitrary"),
        name="diff_attention",
    )(lam, far, qkv, qkv, qkv, bias_tab, subln_g)


def _rel_bucket(rel):
    nb = REL_BUCKETS // 2
    max_exact = nb // 2
    ret = jnp.where(rel > 0, nb, 0)
    n = jnp.abs(rel)
    nf = jnp.maximum(n, 1).astype(F32)
    large = max_exact + (jnp.log(nf / max_exact) / math.log(REL_MAX_DIST / max_exact)
                         * (nb - max_exact)).astype(jnp.int32)
    large = jnp.minimum(large, nb - 1)
    return ret + jnp.where(n < max_exact, n, large)


def _bias_kernel(rb_ref, bucket_ref, o_ref, *, tile):
    h = pl.program_id(0)
    half = REL_BUCKETS // 2
    ranges = ((half - 1, half), (0, half), (0, REL_BUCKETS), (half, REL_BUCKETS), (REL_BUCKETS - 1, REL_BUCKETS))
    rows = 4 * SUBLANES
    for t in range(BIAS_TILES):
        def body(i, carry, t=t):
            r0 = pl.multiple_of(i * rows, rows)
            bk = bucket_ref[t, pl.ds(r0, rows), :]
            acc = jnp.zeros(bk.shape, F32)
            for bkt in range(*ranges[t]):
                acc = jnp.where(bk == bkt, rb_ref[h, bkt], acc)
            o_ref[t, pl.ds(r0, rows), :] = acc
            return carry
        lax.fori_loop(0, tile // rows, body, 0)


def _bias_tables(rel_bias, tile):
    assert tile >= REL_MAX_DIST
    nh = rel_bias.shape[1]
    dd = jnp.arange(-2, 3, dtype=jnp.int32)[:, None, None]
    kk = jnp.arange(tile, dtype=jnp.int32)[None, :, None]
    qq = jnp.arange(tile, dtype=jnp.int32)[None, None, :]
    bucket = _rel_bucket(dd * tile + kk - qq)
    rb = jnp.transpose(rel_bias.astype(F32)) * LOG2E
    far = jnp.stack([rb[:, REL_BUCKETS // 2 - 1], rb[:, REL_BUCKETS - 1]], axis=1)
    tab = pl.pallas_call(
        functools.partial(_bias_kernel, tile=tile),
        out_shape=jax.ShapeDtypeStruct((nh, BIAS_TILES, tile, tile), F32),
        grid=(nh,),
        in_specs=[pl.BlockSpec(memory_space=pltpu.SMEM),
                  pl.BlockSpec((BIAS_TILES, tile, tile), lambda i: (0, 0, 0))],
        out_specs=pl.BlockSpec((None, BIAS_TILES, tile, tile), lambda i: (i, 0, 0, 0)),
        compiler_params=_cparams("parallel"),
        name="rel_bias_tiles",
    )(rb, bucket)
    return tab, far


def _ln(z, g, b):
    mu = jnp.mean(z, axis=-1, keepdims=True)
    zc = z - mu
    var = jnp.mean(zc * zc, axis=-1, keepdims=True)
    return zc * lax.rsqrt(var + LN_EPS) * g + b


def _route_epilogue(xn, wr_hi_ref, wr_lo_ref, br_ref, route_ref, counts_ref, base_ref):
    rows = xn.shape[0]
    x_hi, x_lo = _split_bf16(xn)
    nt = (((1,), (1,)), ((), ()))
    logits = (lax.dot_general(wr_hi_ref[...], x_hi, nt, preferred_element_type=F32)
              + lax.dot_general(wr_hi_ref[...], x_lo, nt, preferred_element_type=F32)
              + lax.dot_general(wr_lo_ref[...], x_hi, nt, preferred_element_type=F32))
    logits = logits + br_ref[...]

    lg = logits[0:FINE_ROW0]
    gmax = jnp.max(lg, axis=0, keepdims=True)
    pg_top = 1.0 / jnp.sum(jnp.exp(lg - gmax), axis=0, keepdims=True)
    gio = lax.broadcasted_iota(jnp.int32, lg.shape, 0).astype(F32)
    g_idx = jnp.min(jnp.where(lg == gmax, gio, float(FINE_ROW0)), axis=0, keepdims=True)

    sel = jnp.zeros((EXPERTS_PER_GROUP, rows), F32)
    for g in range(N_GROUPS):
        r0 = FINE_ROW0 + g * EXPERTS_PER_GROUP
        sel = jnp.where(g_idx == float(g), logits[r0:r0 + EXPERTS_PER_GROUP], sel)
    eio = lax.broadcasted_iota(jnp.int32, sel.shape, 0).astype(F32)
    v1 = jnp.max(sel, axis=0, keepdims=True)
    i1 = jnp.min(jnp.where(sel == v1, eio, float(EXPERTS_PER_GROUP)), axis=0, keepdims=True)
    rest = jnp.where(eio == i1, -jnp.inf, sel)
    v2 = jnp.max(rest, axis=0, keepdims=True)
    i2 = jnp.min(jnp.where(rest == v2, eio, float(EXPERTS_PER_GROUP)), axis=0, keepdims=True)
    e2 = jnp.exp(v2 - v1)
    den = 1.0 + e2
    gate1 = pg_top / den
    gate2 = pg_top * e2 / den
    eid1 = g_idx * float(EXPERTS_PER_GROUP) + i1
    eid2 = g_idx * float(EXPERTS_PER_GROUP) + i2

    xio = lax.broadcasted_iota(jnp.int32, (N_EXPERTS, rows), 0).astype(F32)
    cnt = jnp.where((xio == eid1) | (xio == eid2), 1.0, 0.0)
    base_new = base_ref[...] + jnp.sum(cnt, axis=1, keepdims=True)
    base_ref[...] = base_new
    counts_ref[...] = jnp.broadcast_to(base_new, counts_ref.shape)
    rio = lax.broadcasted_iota(jnp.int32, route_ref.shape, 0)
    out = jnp.zeros(route_ref.shape, F32)
    for r, val in enumerate((eid1, eid2, gate1, gate2)):
        out = jnp.where(rio == r, val, out)
    route_ref[...] = out


def _store_rows(xn, xn_ref, xt_ref):
    xn_ref[...] = xn
    xt_ref[...] = _tiles_from_rows(xn)


def _proj_ln_kernel(a_ref, w_ref, x_ref, g_ref, b_ref, wr_hi_ref, wr_lo_ref, br_ref,
                    xn_ref, xt_ref, route_ref, counts_ref, base_ref):
    @pl.when(pl.program_id(0) == 0)
    def _():
        base_ref[...] = jnp.zeros(base_ref.shape, F32)

    h = jnp.dot(a_ref[...], w_ref[...], preferred_element_type=F32)
    xn = _ln(ALPHA * x_ref[...] + h, g_ref[...], b_ref[...])
    _store_rows(xn, xn_ref, xt_ref)
    _route_epilogue(xn, wr_hi_ref, wr_lo_ref, br_ref, route_ref, counts_ref, base_ref)


def _fft2_ln_kernel(a_ref, g_tab_ref, mr_ref, mi_ref, bo_ref, x_ref, g_ref, b_ref,
                    wr_hi_ref, wr_lo_ref, br_ref,
                    xn_ref, xt_ref, route_ref, counts_ref, base_ref):
    @pl.when(pl.program_id(0) == 0)
    def _():
        base_ref[...] = jnp.zeros(base_ref.shape, F32)

    rows = xn_ref.shape[0]
    d = xn_ref.shape[1]
    kb = g_tab_ref.shape[0]
    s1 = rows // kb
    at = pltpu.einshape("njd->jnd", a_ref[...])
    vr, vi = [], []
    for j in range(kb):
        aj = jnp.concatenate([at[j], at[kb + j]], axis=0)
        vj = jnp.dot(g_tab_ref[j], aj, preferred_element_type=F32)
        vr.append(vj[:s1])
        vi.append(vj[s1:])
    vr = pltpu.einshape("jkd->kjd", jnp.stack(vr)).reshape(rows, d)
    vi = pltpu.einshape("jkd->kjd", jnp.stack(vi)).reshape(rows, d)
    h = (jnp.dot(vr.astype(BF16), mr_ref[...], preferred_element_type=F32)
         + jnp.dot(vi.astype(BF16), mi_ref[...], preferred_element_type=F32)
         + bo_ref[...])
    xn = _ln(ALPHA * x_ref[...].reshape(rows, d) + h, g_ref[...], b_ref[...])
    _store_rows(xn, xn_ref, xt_ref)
    _route_epilogue(xn, wr_hi_ref, wr_lo_ref, br_ref, route_ref, counts_ref, base_ref)


def _router_specs(d, rows):
    const2 = lambda i: (0, 0)
    ins = [pl.BlockSpec((1, d), const2), pl.BlockSpec((1, d), const2),
           pl.BlockSpec((ROUTER_ROWS, d), const2), pl.BlockSpec((ROUTER_ROWS, d), const2),
           pl.BlockSpec((ROUTER_ROWS, 1), const2)]
    outs = [pl.BlockSpec((rows, d), lambda i: (i, 0)),
            pl.BlockSpec((rows, d // LANES, LANES), lambda i: (i, 0, 0)),
            pl.BlockSpec((SUBLANES, rows), lambda i: (0, i)),
            pl.BlockSpec((N_EXPERTS, LANES), const2)]
    return ins, outs


def _router_out_shapes(t, d):
    return (jax.ShapeDtypeStruct((t, d), F32),
            jax.ShapeDtypeStruct((t, d // LANES, LANES), F32),
            jax.ShapeDtypeStruct((SUBLANES, t), F32),
            jax.ShapeDtypeStruct((N_EXPERTS, LANES), F32))


def _proj_ln_router(att, wo_bf16, x2, ln_g, ln_b, router):
    t, d = x2.shape
    rows = ROW_TILE
    tail_in, outs = _router_specs(d, rows)
    return pl.pallas_call(
        _proj_ln_kernel,
        out_shape=_router_out_shapes(t, d),
        grid=(t // rows,),
        in_specs=[pl.BlockSpec((rows, d), lambda i: (i, 0)),
                  pl.BlockSpec((d, d), lambda i: (0, 0)),
                  pl.BlockSpec((rows, d), lambda i: (i, 0))] + tail_in,
        out_specs=outs,
        scratch_shapes=[pltpu.VMEM((N_EXPERTS, 1), F32)],
        compiler_params=_cparams("arbitrary"),
        name="proj_ln_router",
    )(att, wo_bf16, x2, ln_g, ln_b, *router)


def _fft1_kernel(f_ref, x_ref, o_ref):
    xt = pltpu.einshape("njd->jnd", x_ref[...])
    for j in range(xt.shape[0]):
        o_ref[j] = jnp.dot(f_ref[...], xt[j].astype(BF16), preferred_element_type=F32).astype(BF16)


def _fft_stage1(x3):
    b, s, d = x3.shape
    s1 = FFT_S1
    s2 = s // s1
    kb = FFT_KB
    k = np.arange(s2)
    ang = 2.0 * np.pi * ((k[:, None] * k[None, :]) % s2) / s2
    f = np.stack([np.cos(ang), -np.sin(ang)])
    f = f.reshape(2, s2 // kb, kb, s2).transpose(1, 0, 2, 3).reshape(2 * s2, s2)
    f = jnp.asarray(f, dtype=BF16)
    n1_tile = SUBLANES
    return pl.pallas_call(
        _fft1_kernel,
        out_shape=jax.ShapeDtypeStruct((b, s1, 2 * s2, d), BF16),
        grid=(b, s1 // n1_tile),
        in_specs=[pl.BlockSpec((2 * s2, s2), lambda bi, ni: (0, 0)),
                  pl.BlockSpec((None, s2, n1_tile, d), lambda bi, ni: (bi, 0, ni, 0))],
        out_specs=pl.BlockSpec((None, n1_tile, 2 * s2, d), lambda bi, ni: (bi, ni, 0, 0)),
        compiler_params=_cparams("parallel", "parallel"),
        name="fft_stage1",
    )(f, x3.reshape(b, s2, s1, d))


def _fft_stage2_table(s):
    s1 = FFT_S1
    s2 = s // s1
    k2 = np.arange(s2)[:, None, None]
    k1 = np.arange(s1)[None, :, None]
    n1 = np.arange(s1)[None, None, :]
    ang = 2.0 * np.pi * ((k1 * n1 * s2 + k2 * n1) % s) / s
    gr, gi = np.cos(ang), -np.sin(ang)
    g = np.concatenate([np.concatenate([gr, -gi], axis=2), np.concatenate([gi, gr], axis=2)], axis=1)
    return jnp.asarray(g, dtype=BF16)


def _fold_kernel(cs_hi_ref, cs_lo_ref, w_ref, o_ref):
    w_hi, w_lo = _split_bf16(w_ref[...])
    o_ref[...] = (jnp.dot(cs_hi_ref[...], w_hi, preferred_element_type=F32)
                  + jnp.dot(cs_hi_ref[...], w_lo, preferred_element_type=F32)
                  + jnp.dot(cs_lo_ref[...], w_hi, preferred_element_type=F32)).astype(BF16)


def _fold_channel_dft(w_o, s):
    d = w_o.shape[0]
    cg = d // FNET_GROUPS
    c = np.arange(cg)
    ang = 2.0 * np.pi * ((c[:, None] * c[None, :]) % cg) / cg
    scale = 1.0 / math.sqrt(s * cg)
    cs = jnp.asarray(np.stack([np.cos(ang), np.sin(ang)]) * scale, dtype=F32)
    cs_hi, cs_lo = _split_bf16(cs)
    out = pl.pallas_call(
        _fold_kernel,
        out_shape=jax.ShapeDtypeStruct((2, d, d), BF16),
        grid=(2, FNET_GROUPS),
        in_specs=[pl.BlockSpec((None, cg, cg), lambda ci, gi: (ci, 0, 0)),
                  pl.BlockSpec((None, cg, cg), lambda ci, gi: (ci, 0, 0)),
                  pl.BlockSpec((cg, d), lambda ci, gi: (gi, 0))],
        out_specs=pl.BlockSpec((None, cg, d), lambda ci, gi: (ci, gi, 0)),
        compiler_params=_cparams("parallel", "parallel"),
        name="fold_channel_dft",
    )(cs_hi, cs_lo, w_o.astype(F32))
    return out[0], out[1]


def _fft2_ln_router(a1, gbig, mr, mi, b_o, x3, ln_g, ln_b, router):
    b, s, d = x3.shape
    s1 = FFT_S1
    s2 = s // s1
    kb = FFT_KB
    nblk = s2 // kb
    rows = s1 * kb
    t = b * s
    a5 = a1.reshape(b, s1, nblk, 2 * kb, d)
    x5 = x3.reshape(b, s1, nblk, kb, d)
    tail_in, outs = _router_specs(d, rows)
    return pl.pallas_call(
        _fft2_ln_kernel,
        out_shape=_router_out_shapes(t, d),
        grid=(nblk * b,),
        in_specs=[pl.BlockSpec((None, s1, None, 2 * kb, d), lambda i: (i % b, 0, i // b, 0, 0)),
                  pl.BlockSpec((kb, 2 * s1, 2 * s1), lambda i: (i // b, 0, 0)),
                  pl.BlockSpec((d, d), lambda i: (0, 0)),
                  pl.BlockSpec((d, d), lambda i: (0, 0)),
                  pl.BlockSpec((1, d), lambda i: (0, 0)),
                  pl.BlockSpec((None, s1, None, kb, d), lambda i: (i % b, 0, i // b, 0, 0))] + tail_in,
        out_specs=outs,
        scratch_shapes=[pltpu.VMEM((N_EXPERTS, 1), F32)],
        compiler_params=_cparams("arbitrary"),
        name="fft2_ln_router",
    )(a5, gbig, mr, mi, b_o, x5, ln_g, ln_b, *router)


def _rows_from_tiles(ref, *lead):
    xt = pltpu.einshape("rcl->crl", ref[lead] if lead else ref[...])
    return jnp.concatenate([xt[c] for c in range(xt.shape[0])], axis=1)


def _tiles_from_rows(y):
    yt = jnp.stack([y[:, c * LANES:(c + 1) * LANES] for c in range(y.shape[1] // LANES)])
    return pltpu.einshape("crl->rcl", yt)


def _expert_kernel(bexp_ref, nused_ref, src_ref, src_next_ref, dst_ref, xn_ref, w1_ref, w3_ref, w2_ref, out_ref,
                   xbuf, ybuf, w13_bf, w2_bf, gsem, ssem, *, de):
    i = pl.program_id(0)
    last = pl.num_programs(0) - 1
    n_used = nused_ref[0]
    slot = lax.rem(i, 2)
    blk = xbuf.shape[1]
    half = blk // 2

    def gather(idx_ref, sl):
        def body(r, carry):
            for pr in range(2):
                row = pr * half + r
                pltpu.make_async_copy(xn_ref.at[idx_ref[0, 0, row]], xbuf.at[sl, row],
                                      gsem.at[sl]).start(priority=pr)
            return carry
        lax.fori_loop(0, half, body, 0, unroll=8)

    def gather_wait(sl):
        pltpu.make_async_copy(xn_ref.at[pl.ds(0, blk)], xbuf.at[sl], gsem.at[sl]).wait()

    def scatter_wait(sl):
        pltpu.make_async_copy(ybuf.at[sl], out_ref.at[pl.ds(0, blk)], ssem.at[sl]).wait()

    @pl.when(i == 0)
    def _():
        gather(src_ref, 0)

    @pl.when(i + 1 < n_used)
    def _():
        gather(src_next_ref, 1 - slot)

    @pl.when((i == 0) | (bexp_ref[i] != bexp_ref[jnp.maximum(i - 1, 0)]))
    def _():
        w13_bf[:, :de] = w1_ref[...].astype(BF16)
        w13_bf[:, de:] = w3_ref[...].astype(BF16)
        w2_bf[...] = w2_ref[...].astype(BF16)

    @pl.when(i >= 2)
    def _():
        scatter_wait(slot)

    @pl.when(i < n_used)
    def _():
        gather_wait(slot)
        xb = _rows_from_tiles(xbuf, slot).astype(BF16)
        h = jnp.dot(xb, w13_bf[...], preferred_element_type=F32)
        a = h[:, :de]
        act = (a / (1.0 + jnp.exp(-a))) * h[:, de:]
        y = jnp.dot(act.astype(BF16), w2_bf[...], preferred_element_type=F32)
        ybuf[slot] = _tiles_from_rows(y)

        def body(r, carry):
            for pr in range(2):
                row = pr * half + r
                pltpu.make_async_copy(ybuf.at[slot, row], out_ref.at[dst_ref[0, 0, row]],
                                      ssem.at[slot]).start(priority=pr)
            return carry
        lax.fori_loop(0, half, body, 0, unroll=8)

    @pl.when(i >= n_used)
    def _():
        ybuf[slot] = jnp.zeros(ybuf.shape[1:], F32)
        pltpu.make_async_copy(ybuf.at[slot], out_ref.at[pl.ds(dst_ref[0, 0, 0], blk)], ssem.at[slot]).start()

    @pl.when(i == last)
    def _():
        @pl.when(i >= 1)
        def _():
            scatter_wait(1 - slot)
        scatter_wait(slot)


def _expert_mlp(xn_tiles, src_tok, dst_row, layer, w1, w3, w2, block_exp, n_used):
    nb, _, blk = src_tok.shape
    tile_rows = xn_tiles.shape[1]
    d = tile_rows * LANES
    de = w2.shape[2]

    def w_map(i, bexp, nused):
        return (layer, bexp[i], 0, 0)

    smem_block = functools.partial(pl.BlockSpec, (1, 1, blk), memory_space=pltpu.SMEM)
    grid_spec = pltpu.PrefetchScalarGridSpec(
        num_scalar_prefetch=2,
        grid=(nb,),
        in_specs=[smem_block(lambda i, bexp, nused: (i, 0, 0)),
                  smem_block(lambda i, bexp, nused: (jnp.minimum(i + 1, nb - 1), 0, 0)),
                  smem_block(lambda i, bexp, nused: (i, 0, 0)),
                  pl.BlockSpec(memory_space=pl.ANY),
                  pl.BlockSpec((None, None, d, de), w_map),
                  pl.BlockSpec((None, None, d, de), w_map),
                  pl.BlockSpec((None, None, de, d), w_map)],
        out_specs=pl.BlockSpec(memory_space=pl.ANY),
        scratch_shapes=[pltpu.VMEM((2, blk, tile_rows, LANES), F32), pltpu.VMEM((2, blk, tile_rows, LANES), F32),
                        pltpu.VMEM((d, 2 * de), BF16), pltpu.VMEM((de, d), BF16),
                        pltpu.SemaphoreType.DMA((2,)), pltpu.SemaphoreType.DMA((2,))],
    )
    return pl.pallas_call(
        functools.partial(_expert_kernel, de=de),
        out_shape=jax.ShapeDtypeStruct((nb * blk, tile_rows, LANES), F32),
        grid_spec=grid_spec,
        compiler_params=_cparams("arbitrary"),
        name="expert_mlp",
    )(block_exp, n_used, src_tok, src_tok, dst_row, xn_tiles, w1, w3, w2)


def _combine_ln_kernel(y1_ref, y2_ref, x_ref, gates_ref, g_ref, b_ref, o_ref):
    gates = gates_ref[...]
    m = gates[:, 0:1] * _rows_from_tiles(y1_ref) + gates[:, 1:2] * _rows_from_tiles(y2_ref)
    y = _ln(ALPHA * x_ref[...] + m, g_ref[...], b_ref[...])
    o_ref[...] = y.reshape(o_ref.shape)


def _combine_ln(ys, xn, gates, ln_g, ln_b, out_shape, out_spec):
    t, d = xn.shape
    rows = ROW_TILE
    nsteps = t // rows
    tile_rows = ys.shape[1]
    return pl.pallas_call(
        _combine_ln_kernel,
        out_shape=out_shape,
        grid=(nsteps,),
        in_specs=[pl.BlockSpec((rows, tile_rows, LANES), lambda i: (i, 0, 0)),
                  pl.BlockSpec((rows, tile_rows, LANES), lambda i: (nsteps + i, 0, 0)),
                  pl.BlockSpec((rows, d), lambda i: (i, 0)),
                  pl.BlockSpec((rows, 2), lambda i: (i, 0)),
                  pl.BlockSpec((1, d), lambda i: (0, 0)),
                  pl.BlockSpec((1, d), lambda i: (0, 0))],
        out_specs=out_spec,
        compiler_params=_cparams("parallel"),
        name="moe_combine_ln",
    )(ys, ys, xn, gates, ln_g, ln_b)


def _moe_ln(xn, xn_tiles, route, counts, experts_w, ln_g, ln_b, out_shape, out_spec):
    t, d = xn.shape
    blk = EXPERT_BLOCK
    nb = (2 * t) // blk + N_EXPERTS
    p_rows = nb * blk
    n_dummy = p_rows - 2 * t
    eid = route[0:2].astype(jnp.int32).reshape(2 * t)
    cnt = counts[:, 0].astype(jnp.int32)
    padded = (cnt + blk - 1) // blk * blk
    pends = jnp.cumsum(padded)
    pad_ends = jnp.cumsum(padded - cnt)
    dummy = jnp.arange(n_dummy, dtype=jnp.int32) - (p_rows - pends[-1])
    dummy_expert = jnp.where(dummy < 0, N_EXPERTS,
                             jnp.sum((pad_ends[None, :] <= dummy[:, None]).astype(jnp.int32), axis=1))
    group = jnp.concatenate([2 * eid, 2 * dummy_expert + 1])
    slot_asg = lax.sort(group * p_rows + jnp.arange(p_rows, dtype=jnp.int32)) % p_rows
    src_tok = jnp.where(slot_asg < t, slot_asg, jnp.where(slot_asg < 2 * t, slot_asg - t, 0))
    block_row0 = jnp.arange(nb, dtype=jnp.int32)[:, None] * blk
    block_exp = jnp.minimum(jnp.sum((pends[None, :] <= block_row0).astype(jnp.int32), axis=1), N_EXPERTS - 1)
    n_used = (pends[-1:] // blk).astype(jnp.int32)
    gates = jnp.transpose(route[2:4])

    ys = _expert_mlp(xn_tiles, src_tok.reshape(nb, 1, blk), slot_asg.reshape(nb, 1, blk), *experts_w,
                     block_exp, n_used)
    return _combine_ln(ys, xn, gates, ln_g, ln_b, out_shape, out_spec)


def _router_params(w_group, b_group, w_fine, b_fine, rows):
    d = w_group.shape[0]
    w = jnp.zeros((ROUTER_ROWS, d), F32)
    w = w.at[0:N_GROUPS].set(jnp.transpose(w_group).astype(F32))
    w = w.at[FINE_ROW0:FINE_ROW0 + N_EXPERTS].set(jnp.transpose(w_fine).astype(F32))
    bias = jnp.zeros((ROUTER_ROWS, 1), F32).at[N_GROUPS:FINE_ROW0, 0].set(NEG_BIG)
    bias = bias.at[0:N_GROUPS, 0].set(b_group.astype(F32))
    bias = bias.at[FINE_ROW0:FINE_ROW0 + N_EXPERTS, 0].set(b_fine.astype(F32))
    w_hi, w_lo = _split_bf16(w)
    return w_hi, w_lo, bias


def _row(v):
    return v.reshape(1, -1).astype(F32)


def _trunk(x, p, shared):
    b, s, d = x.shape
    t = b * s
    h = d // HEAD_W
    assert s % ATT_TILE == 0 and t % ROW_TILE == 0 and s % (FFT_S1 * FFT_KB) == 0
    assert ROW_TILE == FFT_S1 * FFT_KB
    flat_shape = jax.ShapeDtypeStruct((t, d), F32)
    flat_spec = pl.BlockSpec((ROW_TILE, d), lambda i: (i, 0))
    x2 = x.reshape(t, d)
    for i in range(DEPTH):
        j = i // 2
        if i % 2 == 0:
            qkv = _qkv_proj(x2, shared["wqkv"][j])
            att = _attention(qkv, *shared["bias"], shared["lam"][j], shared["subln_g"][j], b, s, d,
                             shared["lam_init"][j])
            xn, xt, route, counts = _proj_ln_router(att, shared["wo"][j], x2, _row(p["ln1_g"][i]),
                                                    _row(p["ln1_b"][i]), shared["router"][i])
            x2 = _moe_ln(xn, xt, route, counts, shared["experts"][i],
                         _row(p["ln2_g"][i]), _row(p["ln2_b"][i]), flat_shape, flat_spec)
        else:
            x3 = x2.reshape(b, s, d)
            a1 = _fft_stage1(x3)
            mr, mi = shared["fold"][(j, s)]
            xn, xt, route, counts = _fft2_ln_router(a1, shared["gbig"][s], mr, mi, _row(p["fnet_b_o"][j]), x3,
                                                    _row(p["ln1_g"][i]), _row(p["ln1_b"][i]),
                                                    shared["router"][i])
            nblk = s // (FFT_S1 * FFT_KB)
            out_shape = jax.ShapeDtypeStruct((b, FFT_S1, nblk, FFT_KB, d), F32)
            out_spec = pl.BlockSpec((None, FFT_S1, None, FFT_KB, d), lambda i2: (i2 % b, 0, i2 // b, 0, 0))
            y5 = _moe_ln(xn, xt, route, counts, shared["experts"][i],
                         _row(p["ln2_g"][i]), _row(p["ln2_b"][i]), out_shape, out_spec)
            x2 = y5.reshape(t, d)
    return x2.reshape(b, s, d)


def kernel(x_prompt, x_sample, rel_bias, attn_w_qkv, attn_lambda_q1, attn_lambda_k1, attn_lambda_q2, attn_lambda_k2, attn_subln_g, attn_w_o, fnet_w_o, fnet_b_o, ln1_g, ln1_b, ln2_g, ln2_b, moe_w_group, moe_b_group, moe_w_fine, moe_b_fine, moe_w1, moe_w3, moe_w2):
    d = x_prompt.shape[-1]
    p = dict(fnet_b_o=fnet_b_o, ln1_g=ln1_g, ln1_b=ln1_b, ln2_g=ln2_g, ln2_b=ln2_b)
    n_attn = attn_w_qkv.shape[0]
    n_fnet = fnet_w_o.shape[0]
    seqs = sorted({x_prompt.shape[1], x_sample.shape[1]})

    qscale = jnp.concatenate([jnp.full((d,), HEAD_DIM ** -0.5 * LOG2E, F32), jnp.ones((2 * d,), F32)])
    shared = dict(
        wqkv=[(attn_w_qkv[j].astype(F32) * qscale).astype(BF16) for j in range(n_attn)],
        wo=[attn_w_o[j].astype(BF16) for j in range(n_attn)],
        subln_g=[_row(attn_subln_g[j]) for j in range(n_attn)],
        lam_init=[0.8 - 0.6 * math.exp(-0.3 * (2 * j)) for j in range(n_attn)],
        bias=_bias_tables(rel_bias, ATT_TILE),
        router=[_router_params(moe_w_group[i], moe_b_group[i], moe_w_fine[i], moe_b_fine[i], ROW_TILE)
                for i in range(DEPTH)],
        experts=[(i, moe_w1, moe_w3, moe_w2) for i in range(DEPTH)],
        gbig={s: _fft_stage2_table(s) for s in seqs},
        fold={(j, s): _fold_channel_dft(fnet_w_o[j], s) for j in range(n_fnet) for s in seqs},
    )
    shared["lam"] = [
        (jnp.exp(jnp.sum(attn_lambda_q1[j].astype(F32) * attn_lambda_k1[j].astype(F32)))
         - jnp.exp(jnp.sum(attn_lambda_q2[j].astype(F32) * attn_lambda_k2[j].astype(F32)))
         + shared["lam_init"][j]).reshape(1).astype(F32)
        for j in range(n_attn)]

    return (_trunk(x_prompt, p, shared), _trunk(x_sample, p, shared))
```

```python
import functools
import math

import numpy as np
import jax
import jax.numpy as jnp
from jax import lax
from jax.experimental import pallas as pl
from jax.experimental.pallas import tpu as pltpu

F32 = jnp.float32
BF16 = jnp.bfloat16

HEAD_DIM = 64
HEAD_W = 2 * HEAD_DIM
REL_BUCKETS = 32
REL_MAX_DIST = 128
FNET_GROUPS = 4
N_GROUPS = 4
EXPERTS_PER_GROUP = 8
N_EXPERTS = N_GROUPS * EXPERTS_PER_GROUP
DEPTH = 2
ALPHA = (2 * DEPTH) ** 0.25
LN_EPS = 1e-5
LOG2E = 1.4426950408889634

LANES = 128
SUBLANES = 8
VMEM_LIMIT_BYTES = 52 * 1024 * 1024

ROW_TILE = 512
ATT_TILE = 512
BIAS_TILES = 5
EXPERT_BLOCK = 256
FFT_S1 = 64
FFT_KB = SUBLANES
ROUTER_ROWS = 48
FINE_ROW0 = 8
NEG_BIG = -1e30


def _cparams(*sem):
    return pltpu.CompilerParams(dimension_semantics=sem, vmem_limit_bytes=VMEM_LIMIT_BYTES)


def _split_bf16(a):
    hi = a.astype(BF16)
    lo = (a - hi.astype(F32)).astype(BF16)
    return hi, lo


def _qkv_kernel(x_ref, w_ref, o_ref, *, d):
    xb = x_ref[...].astype(BF16)
    for c in range(w_ref.shape[1] // d):
        o_ref[:, c * d:(c + 1) * d] = jnp.dot(
            xb, w_ref[:, c * d:(c + 1) * d], preferred_element_type=F32).astype(BF16)


def _qkv_proj(x2, w_bf16):
    t, d = x2.shape
    n = w_bf16.shape[1]
    return pl.pallas_call(
        functools.partial(_qkv_kernel, d=d),
        out_shape=jax.ShapeDtypeStruct((t, n), BF16),
        grid=(t // ROW_TILE,),
        in_specs=[pl.BlockSpec((ROW_TILE, d), lambda i: (i, 0)),
                  pl.BlockSpec((d, n), lambda i: (0, 0))],
        out_specs=pl.BlockSpec((ROW_TILE, n), lambda i: (i, 0)),
        compiler_params=_cparams("parallel"),
        name="qkv_proj",
    )(x2, w_bf16)


def _attn_kernel(lam_ref, far_ref, q_ref, k_ref, v_ref, bias_ref, g_ref, o_ref,
                 qbd_ref, s_ref, mx_ref, m_ref, l_ref, acc_ref, *, tile, nk, out_scale):
    hi = pl.program_id(1)
    qi = pl.program_id(2)
    cur = lax.rem(qi, 2)
    next_qi = jnp.minimum(qi + 1, nk - 1)

    def build_qbd(qt, q_slot):
        qf = q_ref[pl.ds(pl.multiple_of(qt * tile, tile), tile), :].astype(F32)
        lane = lax.broadcasted_iota(jnp.int32, qf.shape, 1)
        qbd_ref[q_slot, 0:tile, :] = jnp.where(lane < HEAD_DIM, qf, 0.0).astype(BF16)
        qbd_ref[q_slot, tile:2 * tile, :] = jnp.where(lane >= HEAD_DIM, qf, 0.0).astype(BF16)

    m_ref[...] = jnp.full(m_ref.shape, NEG_BIG, F32)
    l_ref[...] = jnp.zeros(l_ref.shape, F32)
    acc_ref[...] = jnp.zeros(acc_ref.shape, F32)
    nt = (((1,), (1,)), ((), ()))
    tn = (((0,), (0,)), ((), ()))
    c_before = far_ref[hi, 0]
    c_after = far_ref[hi, 1]

    npairs = nk // 2 - 1
    near_lo = jnp.clip(qi // 2 - 1, 0, npairs)
    near_hi = jnp.clip(qi // 2 + 1, 0, npairs)

    def scores(kb, slot, with_bias, q_slot=cur, q_tile=qi):
        off = pl.multiple_of(kb * tile, tile)
        k = k_ref[pl.ds(off, tile), :]
        for c in range(2):
            cols = slice(c * tile, (c + 1) * tile)
            s = lax.dot_general(k, qbd_ref[q_slot, cols, :], nt, preferred_element_type=F32)
            if with_bias:
                s = s + bias_ref[jnp.clip(kb - q_tile, -2, 2) + 2]
            s_ref[slot, :, cols] = s
            mx_ref[slot, :, cols] = jnp.max(s, axis=0, keepdims=True)

    def softmax_pv(kb, slot):
        off = pl.multiple_of(kb * tile, tile)
        had_bias = (kb == 0) | (kb == nk - 1) | ((kb >= 2 * near_lo + 1) & (kb <= 2 * near_hi))
        shift = jnp.where(had_bias, 0.0, jnp.where(kb < qi, c_before, c_after))
        v = v_ref[pl.ds(off, tile), :]
        for c in range(2):
            cols = slice(c * tile, (c + 1) * tile)
            m_old = m_ref[:, cols]
            m_new = jnp.maximum(m_old, mx_ref[slot, :, cols] + shift)
            alpha = jnp.exp2(m_old - m_new)
            p = jnp.exp2(s_ref[slot, :, cols] - (m_new - shift))
            l_ref[:, cols] = alpha * l_ref[:, cols] + jnp.sum(p, axis=0, keepdims=True)
            pv = lax.dot_general(v, p.astype(BF16), tn, preferred_element_type=F32)
            acc_ref[:, cols] = acc_ref[:, cols] * alpha + pv
            m_ref[:, cols] = m_new

    def pairs(lo, hi_j, with_bias):
        def body(jj, carry):
            kb = 2 * jj
            scores(kb + 1, 1, with_bias)
            softmax_pv(kb, 0)
            scores(kb + 2, 0, with_bias)
            softmax_pv(kb + 1, 1)
            return carry
        lax.fori_loop(lo, hi_j, body, 0)

    @pl.when(qi == 0)
    def _():
        build_qbd(0, 0)
        scores(0, 0, True, q_slot=0, q_tile=0)

    build_qbd(next_qi, 1 - cur)
    pairs(0, near_lo, False)
    pairs(near_lo, near_hi, True)
    pairs(near_hi, npairs, False)
    scores(nk - 1, 1, True)
    softmax_pv(nk - 2, 0)
    scores(0, 0, True, q_slot=1 - cur, q_tile=next_qi)
    softmax_pv(nk - 1, 1)

    lam = lam_ref[0]
    inv = 1.0 / l_ref[...]
    acc = acc_ref[...]
    o_t = acc[:, :tile] * inv[:, :tile] - lam * (acc[:, tile:] * inv[:, tile:])
    ms = jnp.mean(o_t * o_t, axis=0, keepdims=True)
    o_t = o_t * lax.rsqrt(ms + LN_EPS)
    o_ref[...] = ((o_t.T * g_ref[...]) * out_scale).astype(BF16)


def _attention(qkv, bias_tab, far, lam, subln_g, b, s, d, lam_init):
    h = d // HEAD_W
    tile = ATT_TILE
    nq = s // tile
    assert nq >= 2 and nq % 2 == 0
    kern = functools.partial(_attn_kernel, tile=tile, nk=nq, out_scale=1.0 - lam_init)
    return pl.pallas_call(
        kern,
        out_shape=jax.ShapeDtypeStruct((b * s, d), BF16),
        grid=(b, h, nq),
        in_specs=[
            pl.BlockSpec(memory_space=pltpu.SMEM),
            pl.BlockSpec(memory_space=pltpu.SMEM),
            pl.BlockSpec((s, HEAD_W), lambda bi, hi, qi: (bi, hi)),
            pl.BlockSpec((s, HEAD_W), lambda bi, hi, qi: (bi, h + hi)),
            pl.BlockSpec((s, HEAD_W), lambda bi, hi, qi: (bi, 2 * h + hi)),
            pl.BlockSpec((None, BIAS_TILES, tile, tile), lambda bi, hi, qi: (hi, 0, 0, 0)),
            pl.BlockSpec((1, HEAD_W), lambda bi, hi, qi: (0, 0)),
        ],
        out_specs=pl.BlockSpec((tile, HEAD_W), lambda bi, hi, qi: (bi * nq + qi, hi)),
        scratch_shapes=[pltpu.VMEM((2, 2 * tile, HEAD_W), BF16),
                        pltpu.VMEM((2, tile, 2 * tile), F32),
                        pltpu.VMEM((2, 1, 2 * tile), F32),
                        pltpu.VMEM((1, 2 * tile), F32),
                        pltpu.VMEM((1, 2 * tile), F32),
                        pltpu.VMEM((HEAD_W, 2 * tile), F32)],
        compiler_params=_cparams("parallel", "parallel", "arbitrary"),
        name="diff_attention",
    )(lam, far, qkv, qkv, qkv, bias_tab, subln_g)


def _rel_bucket(rel):
    nb = REL_BUCKETS // 2
    max_exact = nb // 2
    ret = jnp.where(rel > 0, nb, 0)
    n = jnp.abs(rel)
    nf = jnp.maximum(n, 1).astype(F32)
    large = max_exact + (jnp.log(nf / max_exact) / math.log(REL_MAX_DIST / max_exact)
                         * (nb - max_exact)).astype(jnp.int32)
    large = jnp.minimum(large, nb - 1)
    return ret + jnp.where(n < max_exact, n, large)


def _bias_kernel(rb_ref, bucket_ref, o_ref, *, tile):
    h = pl.program_id(0)
    half = REL_BUCKETS // 2
    ranges = ((half - 1, half), (0, half), (0, REL_BUCKETS), (half, REL_BUCKETS), (REL_BUCKETS - 1, REL_BUCKETS))
    rows = 4 * SUBLANES
    for t in range(BIAS_TILES):
        def body(i, carry, t=t):
            r0 = pl.multiple_of(i * rows, rows)
            bk = bucket_ref[t, pl.ds(r0, rows), :]
            acc = jnp.zeros(bk.shape, F32)
            for bkt in range(*ranges[t]):
                acc = jnp.where(bk == bkt, rb_ref[h, bkt], acc)
            o_ref[t, pl.ds(r0, rows), :] = acc
            return carry
        lax.fori_loop(0, tile // rows, body, 0)


def _bias_tables(rel_bias, tile):
    assert tile >= REL_MAX_DIST
    nh = rel_bias.shape[1]
    dd = jnp.arange(-2, 3, dtype=jnp.int32)[:, None, None]
    kk = jnp.arange(tile, dtype=jnp.int32)[None, :, None]
    qq = jnp.arange(tile, dtype=jnp.int32)[None, None, :]
    bucket = _rel_bucket(dd * tile + kk - qq)
    rb = jnp.transpose(rel_bias.astype(F32)) * LOG2E
    far = jnp.stack([rb[:, REL_BUCKETS // 2 - 1], rb[:, REL_BUCKETS - 1]], axis=1)
    tab = pl.pallas_call(
        functools.partial(_bias_kernel, tile=tile),
        out_shape=jax.ShapeDtypeStruct((nh, BIAS_TILES, tile, tile), F32),
        grid=(nh,),
        in_specs=[pl.BlockSpec(memory_space=pltpu.SMEM),
                  pl.BlockSpec((BIAS_TILES, tile, tile), lambda i: (0, 0, 0))],
        out_specs=pl.BlockSpec((None, BIAS_TILES, tile, tile), lambda i: (i, 0, 0, 0)),
        compiler_params=_cparams("parallel"),
        name="rel_bias_tiles",
    )(rb, bucket)
    return tab, far


def _ln(z, g, b):
    mu = jnp.mean(z, axis=-1, keepdims=True)
    zc = z - mu
    var = jnp.mean(zc * zc, axis=-1, keepdims=True)
    return zc * lax.rsqrt(var + LN_EPS) * g + b


def _route_epilogue(xn, wr_hi_ref, wr_lo_ref, br_ref, route_ref, counts_ref, base_ref):
    rows = xn.shape[0]
    x_hi, x_lo = _split_bf16(xn)
    nt = (((1,), (1,)), ((), ()))
    logits = (lax.dot_general(wr_hi_ref[...], x_hi, nt, preferred_element_type=F32)
              + lax.dot_general(wr_hi_ref[...], x_lo, nt, preferred_element_type=F32)
              + lax.dot_general(wr_lo_ref[...], x_hi, nt, preferred_element_type=F32))
    logits = logits + br_ref[...]

    lg = logits[0:FINE_ROW0]
    gmax = jnp.max(lg, axis=0, keepdims=True)
    pg_top = 1.0 / jnp.sum(jnp.exp(lg - gmax), axis=0, keepdims=True)
    gio = lax.broadcasted_iota(jnp.int32, lg.shape, 0).astype(F32)
    g_idx = jnp.min(jnp.where(lg == gmax, gio, float(FINE_ROW0)), axis=0, keepdims=True)

    sel = jnp.zeros((EXPERTS_PER_GROUP, rows), F32)
    for g in range(N_GROUPS):
        r0 = FINE_ROW0 + g * EXPERTS_PER_GROUP
        sel = jnp.where(g_idx == float(g), logits[r0:r0 + EXPERTS_PER_GROUP], sel)
    eio = lax.broadcasted_iota(jnp.int32, sel.shape, 0).astype(F32)
    v1 = jnp.max(sel, axis=0, keepdims=True)
    i1 = jnp.min(jnp.where(sel == v1, eio, float(EXPERTS_PER_GROUP)), axis=0, keepdims=True)
    rest = jnp.where(eio == i1, -jnp.inf, sel)
    v2 = jnp.max(rest, axis=0, keepdims=True)
    i2 = jnp.min(jnp.where(rest == v2, eio, float(EXPERTS_PER_GROUP)), axis=0, keepdims=True)
    e2 = jnp.exp(v2 - v1)
    den = 1.0 + e2
    gate1 = pg_top / den
    gate2 = pg_top * e2 / den
    eid1 = g_idx * float(EXPERTS_PER_GROUP) + i1
    eid2 = g_idx * float(EXPERTS_PER_GROUP) + i2

    xio = lax.broadcasted_iota(jnp.int32, (N_EXPERTS, rows), 0).astype(F32)
    cnt = jnp.where((xio == eid1) | (xio == eid2), 1.0, 0.0)
    base_new = base_ref[...] + jnp.sum(cnt, axis=1, keepdims=True)
    base_ref[...] = base_new
    counts_ref[...] = jnp.broadcast_to(base_new, counts_ref.shape)
    rio = lax.broadcasted_iota(jnp.int32, route_ref.shape, 0)
    out = jnp.zeros(route_ref.shape, F32)
    for r, val in enumerate((eid1, eid2, gate1, gate2)):
        out = jnp.where(rio == r, val, out)
    route_ref[...] = out


def _store_rows(xn, xn_ref, xt_ref):
    xn_ref[...] = xn
    xt_ref[...] = _tiles_from_rows(xn)


def _proj_ln_kernel(a_ref, w_ref, x_ref, g_ref, b_ref, wr_hi_ref, wr_lo_ref, br_ref,
                    xn_ref, xt_ref, route_ref, counts_ref, base_ref):
    @pl.when(pl.program_id(0) == 0)
    def _():
        base_ref[...] = jnp.zeros(base_ref.shape, F32)

    h = jnp.dot(a_ref[...], w_ref[...], preferred_element_type=F32)
    xn = _ln(ALPHA * x_ref[...] + h, g_ref[...], b_ref[...])
    _store_rows(xn, xn_ref, xt_ref)
    _route_epilogue(xn, wr_hi_ref, wr_lo_ref, br_ref, route_ref, counts_ref, base_ref)


def _fft2_ln_kernel(a_ref, g_tab_ref, mr_ref, mi_ref, bo_ref, x_ref, g_ref, b_ref,
                    wr_hi_ref, wr_lo_ref, br_ref,
                    xn_ref, xt_ref, route_ref, counts_ref, base_ref):
    @pl.when(pl.program_id(0) == 0)
    def _():
        base_ref[...] = jnp.zeros(base_ref.shape, F32)

    rows = xn_ref.shape[0]
    d = xn_ref.shape[1]
    kb = g_tab_ref.shape[0]
    s1 = rows // kb
    at = pltpu.einshape("njd->jnd", a_ref[...])
    vr, vi = [], []
    for j in range(kb):
        aj = jnp.concatenate([at[j], at[kb + j]], axis=0)
        vj = jnp.dot(g_tab_ref[j], aj, preferred_element_type=F32)
        vr.append(vj[:s1])
        vi.append(vj[s1:])
    vr = pltpu.einshape("jkd->kjd", jnp.stack(vr)).reshape(rows, d)
    vi = pltpu.einshape("jkd->kjd", jnp.stack(vi)).reshape(rows, d)
    h = (jnp.dot(vr.astype(BF16), mr_ref[...], preferred_element_type=F32)
         + jnp.dot(vi.astype(BF16), mi_ref[...], preferred_element_type=F32)
         + bo_ref[...])
    xn = _ln(ALPHA * x_ref[...].reshape(rows, d) + h, g_ref[...], b_ref[...])
    _store_rows(xn, xn_ref, xt_ref)
    _route_epilogue(xn, wr_hi_ref, wr_lo_ref, br_ref, route_ref, counts_ref, base_ref)


def _router_specs(d, rows):
    const2 = lambda i: (0, 0)
    ins = [pl.BlockSpec((1, d), const2), pl.BlockSpec((1, d), const2),
           pl.BlockSpec((ROUTER_ROWS, d), const2), pl.BlockSpec((ROUTER_ROWS, d), const2),
           pl.BlockSpec((ROUTER_ROWS, 1), const2)]
    outs = [pl.BlockSpec((rows, d), lambda i: (i, 0)),
            pl.BlockSpec((rows, d // LANES, LANES), lambda i: (i, 0, 0)),
            pl.BlockSpec((SUBLANES, rows), lambda i: (0, i)),
            pl.BlockSpec((N_EXPERTS, LANES), const2)]
    return ins, outs


def _router_out_shapes(t, d):
    return (jax.ShapeDtypeStruct((t, d), F32),
            jax.ShapeDtypeStruct((t, d // LANES, LANES), F32),
            jax.ShapeDtypeStruct((SUBLANES, t), F32),
            jax.ShapeDtypeStruct((N_EXPERTS, LANES), F32))


def _proj_ln_router(att, wo_bf16, x2, ln_g, ln_b, router):
    t, d = x2.shape
    rows = ROW_TILE
    tail_in, outs = _router_specs(d, rows)
    return pl.pallas_call(
        _proj_ln_kernel,
        out_shape=_router_out_shapes(t, d),
        grid=(t // rows,),
        in_specs=[pl.BlockSpec((rows, d), lambda i: (i, 0)),
                  pl.BlockSpec((d, d), lambda i: (0, 0)),
                  pl.BlockSpec((rows, d), lambda i: (i, 0))] + tail_in,
        out_specs=outs,
        scratch_shapes=[pltpu.VMEM((N_EXPERTS, 1), F32)],
        compiler_params=_cparams("arbitrary"),
        name="proj_ln_router",
    )(att, wo_bf16, x2, ln_g, ln_b, *router)


def _fft1_kernel(f_ref, x_ref, o_ref):
    xt = pltpu.einshape("njd->jnd", x_ref[...])
    for j in range(xt.shape[0]):
        o_ref[j] = jnp.dot(f_ref[...], xt[j].astype(BF16), preferred_element_type=F32).astype(BF16)


def _fft_stage1(x3):
    b, s, d = x3.shape
    s1 = FFT_S1
    s2 = s // s1
    kb = FFT_KB
    k = np.arange(s2)
    ang = 2.0 * np.pi * ((k[:, None] * k[None, :]) % s2) / s2
    f = np.stack([np.cos(ang), -np.sin(ang)])
    f = f.reshape(2, s2 // kb, kb, s2).transpose(1, 0, 2, 3).reshape(2 * s2, s2)
    f = jnp.asarray(f, dtype=BF16)
    n1_tile = SUBLANES
    return pl.pallas_call(
        _fft1_kernel,
        out_shape=jax.ShapeDtypeStruct((b, s1, 2 * s2, d), BF16),
        grid=(b, s1 // n1_tile),
        in_specs=[pl.BlockSpec((2 * s2, s2), lambda bi, ni: (0, 0)),
                  pl.BlockSpec((None, s2, n1_tile, d), lambda bi, ni: (bi, 0, ni, 0))],
        out_specs=pl.BlockSpec((None, n1_tile, 2 * s2, d), lambda bi, ni: (bi, ni, 0, 0)),
        compiler_params=_cparams("parallel", "parallel"),
        name="fft_stage1",
    )(f, x3.reshape(b, s2, s1, d))


def _fft_stage2_table(s):
    s1 = FFT_S1
    s2 = s // s1
    k2 = np.arange(s2)[:, None, None]
    k1 = np.arange(s1)[None, :, None]
    n1 = np.arange(s1)[None, None, :]
    ang = 2.0 * np.pi * ((k1 * n1 * s2 + k2 * n1) % s) / s
    gr, gi = np.cos(ang), -np.sin(ang)
    g = np.concatenate([np.concatenate([gr, -gi], axis=2), np.concatenate([gi, gr], axis=2)], axis=1)
    return jnp.asarray(g, dtype=BF16)


def _fold_kernel(cs_hi_ref, cs_lo_ref, w_ref, o_ref):
    w_hi, w_lo = _split_bf16(w_ref[...])
    o_ref[...] = (jnp.dot(cs_hi_ref[...], w_hi, preferred_element_type=F32)
                  + jnp.dot(cs_hi_ref[...], w_lo, preferred_element_type=F32)
                  + jnp.dot(cs_lo_ref[...], w_hi, preferred_element_type=F32)).astype(BF16)


def _fold_channel_dft(w_o, s):
    d = w_o.shape[0]
    cg = d // FNET_GROUPS
    c = np.arange(cg)
    ang = 2.0 * np.pi * ((c[:, None] * c[None, :]) % cg) / cg
    scale = 1.0 / math.sqrt(s * cg)
    cs = jnp.asarray(np.stack([np.cos(ang), np.sin(ang)]) * scale, dtype=F32)
    cs_hi, cs_lo = _split_bf16(cs)
    out = pl.pallas_call(
        _fold_kernel,
        out_shape=jax.ShapeDtypeStruct((2, d, d), BF16),
        grid=(2, FNET_GROUPS),
        in_specs=[pl.BlockSpec((None, cg, cg), lambda ci, gi: (ci, 0, 0)),
                  pl.BlockSpec((None, cg, cg), lambda ci, gi: (ci, 0, 0)),
                  pl.BlockSpec((cg, d), lambda ci, gi: (gi, 0))],
        out_specs=pl.BlockSpec((None, cg, d), lambda ci, gi: (ci, gi, 0)),
        compiler_params=_cparams("parallel", "parallel"),
        name="fold_channel_dft",
    )(cs_hi, cs_lo, w_o.astype(F32))
    return out[0], out[1]


def _fft2_ln_router(a1, gbig, mr, mi, b_o, x3, ln_g, ln_b, router):
    b, s, d = x3.shape
    s1 = FFT_S1
    s2 = s // s1
    kb = FFT_KB
    nblk = s2 // kb
    rows = s1 * kb
    t = b * s
    a5 = a1.reshape(b, s1, nblk, 2 * kb, d)
    x5 = x3.reshape(b, s1, nblk, kb, d)
    tail_in, outs = _router_specs(d, rows)
    return pl.pallas_call(
        _fft2_ln_kernel,
        out_shape=_router_out_shapes(t, d),
        grid=(nblk * b,),
        in_specs=[pl.BlockSpec((None, s1, None, 2 * kb, d), lambda i: (i % b, 0, i // b, 0, 0)),
                  pl.BlockSpec((kb, 2 * s1, 2 * s1), lambda i: (i // b, 0, 0)),
                  pl.BlockSpec((d, d), lambda i: (0, 0)),
                  pl.BlockSpec((d, d), lambda i: (0, 0)),
                  pl.BlockSpec((1, d), lambda i: (0, 0)),
                  pl.BlockSpec((None, s1, None, kb, d), lambda i: (i % b, 0, i // b, 0, 0))] + tail_in,
        out_specs=outs,
        scratch_shapes=[pltpu.VMEM((N_EXPERTS, 1), F32)],
        compiler_params=_cparams("arbitrary"),
        name="fft2_ln_router",
    )(a5, gbig, mr, mi, b_o, x5, ln_g, ln_b, *router)


def _rows_from_tiles(ref, *lead):
    xt = pltpu.einshape("rcl->crl", ref[lead] if lead else ref[...])
    return jnp.concatenate([xt[c] for c in range(xt.shape[0])], axis=1)


def _tiles_from_rows(y):
    yt = jnp.stack([y[:, c * LANES:(c + 1) * LANES] for c in range(y.shape[1] // LANES)])
    return pltpu.einshape("crl->rcl", yt)


def _expert_kernel(bexp_ref, nused_ref, src_ref, src_next_ref, dst_ref, xn_ref, w1_ref, w3_ref, w2_ref, out_ref,
                   xbuf, ybuf, w13_bf, w2_bf, gsem, ssem, *, de):
    i = pl.program_id(0)
    last = pl.num_programs(0) - 1
    n_used = nused_ref[0]
    slot = lax.rem(i, 2)
    blk = xbuf.shape[1]
    half = blk // 2

    def gather(idx_ref, sl):
        def body(r, carry):
            for pr in range(2):
                row = pr * half + r
                pltpu.make_async_copy(xn_ref.at[idx_ref[0, 0, row]], xbuf.at[sl, row],
                                      gsem.at[sl]).start(priority=pr)
            return carry
        lax.fori_loop(0, half, body, 0, unroll=8)

    def gather_wait(sl):
        pltpu.make_async_copy(xn_ref.at[pl.ds(0, blk)], xbuf.at[sl], gsem.at[sl]).wait()

    def scatter_wait(sl):
        pltpu.make_async_copy(ybuf.at[sl], out_ref.at[pl.ds(0, blk)], ssem.at[sl]).wait()

    @pl.when(i == 0)
    def _():
        gather(src_ref, 0)

    @pl.when(i + 1 < n_used)
    def _():
        gather(src_next_ref, 1 - slot)

    @pl.when((i == 0) | (bexp_ref[i] != bexp_ref[jnp.maximum(i - 1, 0)]))
    def _():
        w13_bf[:, :de] = w1_ref[...].astype(BF16)
        w13_bf[:, de:] = w3_ref[...].astype(BF16)
        w2_bf[...] = w2_ref[...].astype(BF16)

    @pl.when(i >= 2)
    def _():
        scatter_wait(slot)

    @pl.when(i < n_used)
    def _():
        gather_wait(slot)
        xb = _rows_from_tiles(xbuf, slot).astype(BF16)
        h = jnp.dot(xb, w13_bf[...], preferred_element_type=F32)
        a = h[:, :de]
        act = (a / (1.0 + jnp.exp(-a))) * h[:, de:]
        y = jnp.dot(act.astype(BF16), w2_bf[...], preferred_element_type=F32)
        ybuf[slot] = _tiles_from_rows(y)

        def body(r, carry):
            for pr in range(2):
                row = pr * half + r
                pltpu.make_async_copy(ybuf.at[slot, row], out_ref.at[dst_ref[0, 0, row]],
                                      ssem.at[slot]).start(priority=pr)
            return carry
        lax.fori_loop(0, half, body, 0, unroll=8)

    @pl.when(i >= n_used)
    def _():
        ybuf[slot] = jnp.zeros(ybuf.shape[1:], F32)
        pltpu.make_async_copy(ybuf.at[slot], out_ref.at[pl.ds(dst_ref[0, 0, 0], blk)], ssem.at[slot]).start()

    @pl.when(i == last)
    def _():
        @pl.when(i >= 1)
        def _():
            scatter_wait(1 - slot)
        scatter_wait(slot)


def _expert_mlp(xn_tiles, src_tok, dst_row, layer, w1, w3, w2, block_exp, n_used):
    nb, _, blk = src_tok.shape
    tile_rows = xn_tiles.shape[1]
    d = tile_rows * LANES
    de = w2.shape[2]

    def w_map(i, bexp, nused):
        return (layer, bexp[i], 0, 0)

    smem_block = functools.partial(pl.BlockSpec, (1, 1, blk), memory_space=pltpu.SMEM)
    grid_spec = pltpu.PrefetchScalarGridSpec(
        num_scalar_prefetch=2,
        grid=(nb,),
        in_specs=[smem_block(lambda i, bexp, nused: (i, 0, 0)),
                  smem_block(lambda i, bexp, nused: (jnp.minimum(i + 1, nb - 1), 0, 0)),
                  smem_block(lambda i, bexp, nused: (i, 0, 0)),
                  pl.BlockSpec(memory_space=pl.ANY),
                  pl.BlockSpec((None, None, d, de), w_map),
                  pl.BlockSpec((None, None, d, de), w_map),
                  pl.BlockSpec((None, None, de, d), w_map)],
        out_specs=pl.BlockSpec(memory_space=pl.ANY),
        scratch_shapes=[pltpu.VMEM((2, blk, tile_rows, LANES), F32), pltpu.VMEM((2, blk, tile_rows, LANES), F32),
                        pltpu.VMEM((d, 2 * de), BF16), pltpu.VMEM((de, d), BF16),
                        pltpu.SemaphoreType.DMA((2,)), pltpu.SemaphoreType.DMA((2,))],
    )
    return pl.pallas_call(
        functools.partial(_expert_kernel, de=de),
        out_shape=jax.ShapeDtypeStruct((nb * blk, tile_rows, LANES), F32),
        grid_spec=grid_spec,
        compiler_params=_cparams("arbitrary"),
        name="expert_mlp",
    )(block_exp, n_used, src_tok, src_tok, dst_row, xn_tiles, w1, w3, w2)


def _combine_ln_kernel(y1_ref, y2_ref, x_ref, gates_ref, g_ref, b_ref, o_ref):
    gates = gates_ref[...]
    m = gates[:, 0:1] * _rows_from_tiles(y1_ref) + gates[:, 1:2] * _rows_from_tiles(y2_ref)
    y = _ln(ALPHA * x_ref[...] + m, g_ref[...], b_ref[...])
    o_ref[...] = y.reshape(o_ref.shape)


def _combine_ln(ys, xn, gates, ln_g, ln_b, out_shape, out_spec):
    t, d = xn.shape
    rows = ROW_TILE
    nsteps = t // rows
    tile_rows = ys.shape[1]
    return pl.pallas_call(
        _combine_ln_kernel,
        out_shape=out_shape,
        grid=(nsteps,),
        in_specs=[pl.BlockSpec((rows, tile_rows, LANES), lambda i: (i, 0, 0)),
                  pl.BlockSpec((rows, tile_rows, LANES), lambda i: (nsteps + i, 0, 0)),
                  pl.BlockSpec((rows, d), lambda i: (i, 0)),
                  pl.BlockSpec((rows, 2), lambda i: (i, 0)),
                  pl.BlockSpec((1, d), lambda i: (0, 0)),
                  pl.BlockSpec((1, d), lambda i: (0, 0))],
        out_specs=out_spec,
        compiler_params=_cparams("parallel"),
        name="moe_combine_ln",
    )(ys, ys, xn, gates, ln_g, ln_b)


def _moe_ln(xn, xn_tiles, route, counts, experts_w, ln_g, ln_b, out_shape, out_spec):
    t, d = xn.shape
    blk = EXPERT_BLOCK
    nb = (2 * t) // blk + N_EXPERTS
    p_rows = nb * blk
    n_dummy = p_rows - 2 * t
    eid = route[0:2].astype(jnp.int32).reshape(2 * t)
    cnt = counts[:, 0].astype(jnp.int32)
    padded = (cnt + blk - 1) // blk * blk
    pends = jnp.cumsum(padded)
    pad_ends = jnp.cumsum(padded - cnt)
    dummy = jnp.arange(n_dummy, dtype=jnp.int32) - (p_rows - pends[-1])
    dummy_expert = jnp.where(dummy < 0, N_EXPERTS,
                             jnp.sum((pad_ends[None, :] <= dummy[:, None]).astype(jnp.int32), axis=1))
    group = jnp.concatenate([2 * eid, 2 * dummy_expert + 1])
    slot_asg = lax.sort(group * p_rows + jnp.arange(p_rows, dtype=jnp.int32)) % p_rows
    src_tok = jnp.where(slot_asg < t, slot_asg, jnp.where(slot_asg < 2 * t, slot_asg - t, 0))
    block_row0 = jnp.arange(nb, dtype=jnp.int32)[:, None] * blk
    block_exp = jnp.minimum(jnp.sum((pends[None, :] <= block_row0).astype(jnp.int32), axis=1), N_EXPERTS - 1)
    n_used = (pends[-1:] // blk).astype(jnp.int32)
    gates = jnp.transpose(route[2:4])

    ys = _expert_mlp(xn_tiles, src_tok.reshape(nb, 1, blk), slot_asg.reshape(nb, 1, blk), *experts_w,
                     block_exp, n_used)
    return _combine_ln(ys, xn, gates, ln_g, ln_b, out_shape, out_spec)


def _router_params(w_group, b_group, w_fine, b_fine, rows):
    d = w_group.shape[0]
    w = jnp.zeros((ROUTER_ROWS, d), F32)
    w = w.at[0:N_GROUPS].set(jnp.transpose(w_group).astype(F32))
    w = w.at[FINE_ROW0:FINE_ROW0 + N_EXPERTS].set(jnp.transpose(w_fine).astype(F32))
    bias = jnp.zeros((ROUTER_ROWS, 1), F32).at[N_GROUPS:FINE_ROW0, 0].set(NEG_BIG)
    bias = bias.at[0:N_GROUPS, 0].set(b_group.astype(F32))
    bias = bias.at[FINE_ROW0:FINE_ROW0 + N_EXPERTS, 0].set(b_fine.astype(F32))
    w_hi, w_lo = _split_bf16(w)
    return w_hi, w_lo, bias


def _row(v):
    return v.reshape(1, -1).astype(F32)


def _trunk(x, p, shared):
    b, s, d = x.shape
    t = b * s
    h = d // HEAD_W
    assert s % ATT_TILE == 0 and t % ROW_TILE == 0 and s % (FFT_S1 * FFT_KB) == 0
    assert ROW_TILE == FFT_S1 * FFT_KB
    flat_shape = jax.ShapeDtypeStruct((t, d), F32)
    flat_spec = pl.BlockSpec((ROW_TILE, d), lambda i: (i, 0))
    x2 = x.reshape(t, d)
    for i in range(DEPTH):
        j = i // 2
        if i % 2 == 0:
            qkv = _qkv_proj(x2, shared["wqkv"][j])
            att = _attention(qkv, *shared["bias"], shared["lam"][j], shared["subln_g"][j], b, s, d,
                             shared["lam_init"][j])
            xn, xt, route, counts = _proj_ln_router(att, shared["wo"][j], x2, _row(p["ln1_g"][i]),
                                                    _row(p["ln1_b"][i]), shared["router"][i])
            x2 = _moe_ln(xn, xt, route, counts, shared["experts"][i],
                         _row(p["ln2_g"][i]), _row(p["ln2_b"][i]), flat_shape, flat_spec)
        else:
            x3 = x2.reshape(b, s, d)
            a1 = _fft_stage1(x3)
            mr, mi = shared["fold"][(j, s)]
            xn, xt, route, counts = _fft2_ln_router(a1, shared["gbig"][s], mr, mi, _row(p["fnet_b_o"][j]), x3,
                                                    _row(p["ln1_g"][i]), _row(p["ln1_b"][i]),
                                                    shared["router"][i])
            nblk = s // (FFT_S1 * FFT_KB)
            out_shape = jax.ShapeDtypeStruct((b, FFT_S1, nblk, FFT_KB, d), F32)
            out_spec = pl.BlockSpec((None, FFT_S1, None, FFT_KB, d), lambda i2: (i2 % b, 0, i2 // b, 0, 0))
            y5 = _moe_ln(xn, xt, route, counts, shared["experts"][i],
                         _row(p["ln2_g"][i]), _row(p["ln2_b"][i]), out_shape, out_spec)
            x2 = y5.reshape(t, d)
    return x2.reshape(b, s, d)


def kernel(x_prompt, x_sample, rel_bias, attn_w_qkv, attn_lambda_q1, attn_lambda_k1, attn_lambda_q2, attn_lambda_k2, attn_subln_g, attn_w_o, fnet_w_o, fnet_b_o, ln1_g, ln1_b, ln2_g, ln2_b, moe_w_group, moe_b_group, moe_w_fine, moe_b_fine, moe_w1, moe_w3, moe_w2):
    d = x_prompt.shape[-1]
    p = dict(fnet_b_o=fnet_b_o, ln1_g=ln1_g, ln1_b=ln1_b, ln2_g=ln2_g, ln2_b=ln2_b)
    n_attn = attn_w_qkv.shape[0]
    n_fnet = fnet_w_o.shape[0]
    seqs = sorted({x_prompt.shape[1], x_sample.shape[1]})

    qscale = jnp.concatenate([jnp.full((d,), HEAD_DIM ** -0.5 * LOG2E, F32), jnp.ones((2 * d,), F32)])
    shared = dict(
        wqkv=[(attn_w_qkv[j].astype(F32) * qscale).astype(BF16) for j in range(n_attn)],
        wo=[attn_w_o[j].astype(BF16) for j in range(n_attn)],
        subln_g=[_row(attn_subln_g[j]) for j in range(n_attn)],
        lam_init=[0.8 - 0.6 * math.exp(-0.3 * (2 * j)) for j in range(n_attn)],
        bias=_bias_tables(rel_bias, ATT_TILE),
        router=[_router_params(moe_w_group[i], moe_b_group[i], moe_w_fine[i], moe_b_fine[i], ROW_TILE)
                for i in range(DEPTH)],
        experts=[(i, moe_w1, moe_w3, moe_w2) for i in range(DEPTH)],
        gbig={s: _fft_stage2_table(s) for s in seqs},
        fold={(j, s): _fold_channel_dft(fnet_w_o[j], s) for j in range(n_fnet) for s in seqs},
    )
    shared["lam"] = [
        (jnp.exp(jnp.sum(attn_lambda_q1[j].astype(F32) * attn_lambda_k1[j].astype(F32)))
         - jnp.exp(jnp.sum(attn_lambda_q2[j].astype(F32) * attn_lambda_k2[j].astype(F32)))
         + shared["lam_init"][j]).reshape(1).astype(F32)
        for j in range(n_attn)]

    return (_trunk(x_prompt, p, shared), _trunk(x_sample, p, shared))
```

```python
import functools
import math

import numpy as np
import jax
import jax.numpy as jnp
from jax import lax
from jax.experimental import pallas as pl
from jax.experimental.pallas import tpu as pltpu

F32 = jnp.float32
BF16 = jnp.bfloat16

HEAD_DIM = 64
HEAD_W = 2 * HEAD_DIM
REL_BUCKETS = 32
REL_MAX_DIST = 128
FNET_GROUPS = 4
N_GROUPS = 4
EXPERTS_PER_GROUP = 8
N_EXPERTS = N_GROUPS * EXPERTS_PER_GROUP
DEPTH = 2
ALPHA = (2 * DEPTH) ** 0.25
LN_EPS = 1e-5
LOG2E = 1.4426950408889634

LANES = 128
SUBLANES = 8
VMEM_LIMIT_BYTES = 52 * 1024 * 1024

ROW_TILE = 512
ATT_TILE = 512
BIAS_TILES = 5
EXPERT_BLOCK = 256
FFT_S1 = 64
FFT_KB = SUBLANES
ROUTER_ROWS = 48
FINE_ROW0 = 8
NEG_BIG = -1e30


def _cparams(*sem):
    return pltpu.CompilerParams(dimension_semantics=sem, vmem_limit_bytes=VMEM_LIMIT_BYTES)


def _split_bf16(a):
    hi = a.astype(BF16)
    lo = (a - hi.astype(F32)).astype(BF16)
    return hi, lo


def _qkv_kernel(x_ref, w_ref, o_ref, *, d):
    xb = x_ref[...].astype(BF16)
    for c in range(w_ref.shape[1] // d):
        o_ref[:, c * d:(c + 1) * d] = jnp.dot(
            xb, w_ref[:, c * d:(c + 1) * d], preferred_element_type=F32).astype(BF16)


def _qkv_proj(x2, w_bf16):
    t, d = x2.shape
    n = w_bf16.shape[1]
    return pl.pallas_call(
        functools.partial(_qkv_kernel, d=d),
        out_shape=jax.ShapeDtypeStruct((t, n), BF16),
        grid=(t // ROW_TILE,),
        in_specs=[pl.BlockSpec((ROW_TILE, d), lambda i: (i, 0)),
                  pl.BlockSpec((d, n), lambda i: (0, 0))],
        out_specs=pl.BlockSpec((ROW_TILE, n), lambda i: (i, 0)),
        compiler_params=_cparams("parallel"),
        name="qkv_proj",
    )(x2, w_bf16)


def _attn_kernel(lam_ref, far_ref, q_ref, k_ref, v_ref, bias_ref, g_ref, o_ref,
                 qbd_ref, s_ref, mx_ref, m_ref, l_ref, acc_ref, *, tile, nk, out_scale):
    hi = pl.program_id(1)
    qi = pl.program_id(2)
    cur = lax.rem(qi, 2)
    next_qi = jnp.minimum(qi + 1, nk - 1)

    def build_qbd(qt, q_slot):
        qf = q_ref[pl.ds(pl.multiple_of(qt * tile, tile), tile), :].astype(F32)
        lane = lax.broadcasted_iota(jnp.int32, qf.shape, 1)
        qbd_ref[q_slot, 0:tile, :] = jnp.where(lane < HEAD_DIM, qf, 0.0).astype(BF16)
        qbd_ref[q_slot, tile:2 * tile, :] = jnp.where(lane >= HEAD_DIM, qf, 0.0).astype(BF16)

    m_ref[...] = jnp.full(m_ref.shape, NEG_BIG, F32)
    l_ref[...] = jnp.zeros(l_ref.shape, F32)
    acc_ref[...] = jnp.zeros(acc_ref.shape, F32)
    nt = (((1,), (1,)), ((), ()))
    tn = (((0,), (0,)), ((), ()))
    c_before = far_ref[hi, 0]
    c_after = far_ref[hi, 1]

    npairs = nk // 2 - 1
    near_lo = jnp.clip(qi // 2 - 1, 0, npairs)
    near_hi = jnp.clip(qi // 2 + 1, 0, npairs)

    def scores(kb, slot, with_bias, q_slot=cur, q_tile=qi):
        off = pl.multiple_of(kb * tile, tile)
        k = k_ref[pl.ds(off, tile), :]
        for c in range(2):
            cols = slice(c * tile, (c + 1) * tile)
            s = lax.dot_general(k, qbd_ref[q_slot, cols, :], nt, preferred_element_type=F32)
            if with_bias:
                s = s + bias_ref[jnp.clip(kb - q_tile, -2, 2) + 2]
            s_ref[slot, :, cols] = s
            mx_ref[slot, :, cols] = jnp.max(s, axis=0, keepdims=True)

    def softmax_pv(kb, slot):
        off = pl.multiple_of(kb * tile, tile)
        had_bias = (kb == 0) | (kb == nk - 1) | ((kb >= 2 * near_lo + 1) & (kb <= 2 * near_hi))
        shift = jnp.where(had_bias, 0.0, jnp.where(kb < qi, c_before, c_after))
        v = v_ref[pl.ds(off, tile), :]
        for c in range(2):
            cols = slice(c * tile, (c + 1) * tile)
            m_old = m_ref[:, cols]
            m_new = jnp.maximum(m_old, mx_ref[slot, :, cols] + shift)
            alpha = jnp.exp2(m_old - m_new)
            p = jnp.exp2(s_ref[slot, :, cols] - (m_new - shift))
            l_ref[:, cols] = alpha * l_ref[:, cols] + jnp.sum(p, axis=0, keepdims=True)
            pv = lax.dot_general(v, p.astype(BF16), tn, preferred_element_type=F32)
            acc_ref[:, cols] = acc_ref[:, cols] * alpha + pv
            m_ref[:, cols] = m_new

    def pairs(lo, hi_j, with_bias):
        def body(jj, carry):
            kb = 2 * jj
            scores(kb + 1, 1, with_bias)
            softmax_pv(kb, 0)
            scores(kb + 2, 0, with_bias)
            softmax_pv(kb + 1, 1)
            return carry
        lax.fori_loop(lo, hi_j, body, 0)

    @pl.when(qi == 0)
    def _():
        build_qbd(0, 0)
        scores(0, 0, True, q_slot=0, q_tile=0)

    build_qbd(next_qi, 1 - cur)
    pairs(0, near_lo, False)
    pairs(near_lo, near_hi, True)
    pairs(near_hi, npairs, False)
    scores(nk - 1, 1, True)
    softmax_pv(nk - 2, 0)
    scores(0, 0, True, q_slot=1 - cur, q_tile=next_qi)
    softmax_pv(nk - 1, 1)

    lam = lam_ref[0]
    inv = 1.0 / l_ref[...]
    acc = acc_ref[...]
    o_t = acc[:, :tile] * inv[:, :tile] - lam * (acc[:, tile:] * inv[:, tile:])
    ms = jnp.mean(o_t * o_t, axis=0, keepdims=True)
    o_t = o_t * lax.rsqrt(ms + LN_EPS)
    o_ref[...] = ((o_t.T * g_ref[...]) * out_scale).astype(BF16)


def _attention(qkv, bias_tab, far, lam, subln_g, b, s, d, lam_init):
    h = d // HEAD_W
    tile = ATT_TILE
    nq = s // tile
    assert nq >= 2 and nq % 2 == 0
    kern = functools.partial(_attn_kernel, tile=tile, nk=nq, out_scale=1.0 - lam_init)
    return pl.pallas_call(
        kern,
        out_shape=jax.ShapeDtypeStruct((b * s, d), BF16),
        grid=(b, h, nq),
        in_specs=[
            pl.BlockSpec(memory_space=pltpu.SMEM),
            pl.BlockSpec(memory_space=pltpu.SMEM),
            pl.BlockSpec((s, HEAD_W), lambda bi, hi, qi: (bi, hi)),
            pl.BlockSpec((s, HEAD_W), lambda bi, hi, qi: (bi, h + hi)),
            pl.BlockSpec((s, HEAD_W), lambda bi, hi, qi: (bi, 2 * h + hi)),
            pl.BlockSpec((None, BIAS_TILES, tile, tile), lambda bi, hi, qi: (hi, 0, 0, 0)),
            pl.BlockSpec((1, HEAD_W), lambda bi, hi, qi: (0, 0)),
        ],
        out_specs=pl.BlockSpec((tile, HEAD_W), lambda bi, hi, qi: (bi * nq + qi, hi)),
        scratch_shapes=[pltpu.VMEM((2, 2 * tile, HEAD_W), BF16),
                        pltpu.VMEM((2, tile, 2 * tile), F32),
                        pltpu.VMEM((2, 1, 2 * tile), F32),
                        pltpu.VMEM((1, 2 * tile), F32),
                        pltpu.VMEM((1, 2 * tile), F32),
                        pltpu.VMEM((HEAD_W, 2 * tile), F32)],
        compiler_params=_cparams("parallel", "parallel", "arbitrary"),
        name="diff_attention",
    )(lam, far, qkv, qkv, qkv, bias_tab, subln_g)


def _rel_bucket(rel):
    nb = REL_BUCKETS // 2
    max_exact = nb // 2
    ret = jnp.where(rel > 0, nb, 0)
    n = jnp.abs(rel)
    nf = jnp.maximum(n, 1).astype(F32)
    large = max_exact + (jnp.log(nf / max_exact) / math.log(REL_MAX_DIST / max_exact)
                         * (nb - max_exact)).astype(jnp.int32)
    large = jnp.minimum(large, nb - 1)
    return ret + jnp.where(n < max_exact, n, large)


def _bias_kernel(rb_ref, bucket_ref, o_ref, *, tile):
    h = pl.program_id(0)
    half = REL_BUCKETS // 2
    ranges = ((half - 1, half), (0, half), (0, REL_BUCKETS), (half, REL_BUCKETS), (REL_BUCKETS - 1, REL_BUCKETS))
    rows = 4 * SUBLANES
    for t in range(BIAS_TILES):
        def body(i, carry, t=t):
            r0 = pl.multiple_of(i * rows, rows)
            bk = bucket_ref[t, pl.ds(r0, rows), :]
            acc = jnp.zeros(bk.shape, F32)
            for bkt in range(*ranges[t]):
                acc = jnp.where(bk == bkt, rb_ref[h, bkt], acc)
            o_ref[t, pl.ds(r0, rows), :] = acc
            return carry
        lax.fori_loop(0, tile // rows, body, 0)


def _bias_tables(rel_bias, tile):
    assert tile >= REL_MAX_DIST
    nh = rel_bias.shape[1]
    dd = jnp.arange(-2, 3, dtype=jnp.int32)[:, None, None]
    kk = jnp.arange(tile, dtype=jnp.int32)[None, :, None]
    qq = jnp.arange(tile, dtype=jnp.int32)[None, None, :]
    bucket = _rel_bucket(dd * tile + kk - qq)
    rb = jnp.transpose(rel_bias.astype(F32)) * LOG2E
    far = jnp.stack([rb[:, REL_BUCKETS // 2 - 1], rb[:, REL_BUCKETS - 1]], axis=1)
    tab = pl.pallas_call(
        functools.partial(_bias_kernel, tile=tile),
        out_shape=jax.ShapeDtypeStruct((nh, BIAS_TILES, tile, tile), F32),
        grid=(nh,),
        in_specs=[pl.BlockSpec(memory_space=pltpu.SMEM),
                  pl.BlockSpec((BIAS_TILES, tile, tile), lambda i: (0, 0, 0))],
        out_specs=pl.BlockSpec((None, BIAS_TILES, tile, tile), lambda i: (i, 0, 0, 0)),
        compiler_params=_cparams("parallel"),
        name="rel_bias_tiles",
    )(rb, bucket)
    return tab, far


def _ln(z, g, b):
    mu = jnp.mean(z, axis=-1, keepdims=True)
    zc = z - mu
    var = jnp.mean(zc * zc, axis=-1, keepdims=True)
    return zc * lax.rsqrt(var + LN_EPS) * g + b


def _route_epilogue(xn, wr_hi_ref, wr_lo_ref, br_ref, route_ref, counts_ref, base_ref):
    rows = xn.shape[0]
    x_hi, x_lo = _split_bf16(xn)
    nt = (((1,), (1,)), ((), ()))
    logits = (lax.dot_general(wr_hi_ref[...], x_hi, nt, preferred_element_type=F32)
              + lax.dot_general(wr_hi_ref[...], x_lo, nt, preferred_element_type=F32)
              + lax.dot_general(wr_lo_ref[...], x_hi, nt, preferred_element_type=F32))
    logits = logits + br_ref[...]

    lg = logits[0:FINE_ROW0]
    gmax = jnp.max(lg, axis=0, keepdims=True)
    pg_top = 1.0 / jnp.sum(jnp.exp(lg - gmax), axis=0, keepdims=True)
    gio = lax.broadcasted_iota(jnp.int32, lg.shape, 0).astype(F32)
    g_idx = jnp.min(jnp.where(lg == gmax, gio, float(FINE_ROW0)), axis=0, keepdims=True)

    sel = jnp.zeros((EXPERTS_PER_GROUP, rows), F32)
    for g in range(N_GROUPS):
        r0 = FINE_ROW0 + g * EXPERTS_PER_GROUP
        sel = jnp.where(g_idx == float(g), logits[r0:r0 + EXPERTS_PER_GROUP], sel)
    eio = lax.broadcasted_iota(jnp.int32, sel.shape, 0).astype(F32)
    v1 = jnp.max(sel, axis=0, keepdims=True)
    i1 = jnp.min(jnp.where(sel == v1, eio, float(EXPERTS_PER_GROUP)), axis=0, keepdims=True)
    rest = jnp.where(eio == i1, -jnp.inf, sel)
    v2 = jnp.max(rest, axis=0, keepdims=True)
    i2 = jnp.min(jnp.where(rest == v2, eio, float(EXPERTS_PER_GROUP)), axis=0, keepdims=True)
    e2 = jnp.exp(v2 - v1)
    den = 1.0 + e2
    gate1 = pg_top / den
    gate2 = pg_top * e2 / den
    eid1 = g_idx * float(EXPERTS_PER_GROUP) + i1
    eid2 = g_idx * float(EXPERTS_PER_GROUP) + i2

    xio = lax.broadcasted_iota(jnp.int32, (N_EXPERTS, rows), 0).astype(F32)
    cnt = jnp.where((xio == eid1) | (xio == eid2), 1.0, 0.0)
    base_new = base_ref[...] + jnp.sum(cnt, axis=1, keepdims=True)
    base_ref[...] = base_new
    counts_ref[...] = jnp.broadcast_to(base_new, counts_ref.shape)
    rio = lax.broadcasted_iota(jnp.int32, route_ref.shape, 0)
    out = jnp.zeros(route_ref.shape, F32)
    for r, val in enumerate((eid1, eid2, gate1, gate2)):
        out = jnp.where(rio == r, val, out)
    route_ref[...] = out


def _proj_ln_kernel(a_ref, w_ref, x_ref, g_ref, b_ref, wr_hi_ref, wr_lo_ref, br_ref,
                    xt_ref, route_ref, counts_ref, base_ref):
    @pl.when(pl.program_id(0) == 0)
    def _():
        base_ref[...] = jnp.zeros(base_ref.shape, F32)

    h = jnp.dot(a_ref[...], w_ref[...], preferred_element_type=F32)
    xn = _ln(ALPHA * x_ref[...] + h, g_ref[...], b_ref[...])
    xt_ref[...] = _tiles_from_rows(xn)
    _route_epilogue(xn, wr_hi_ref, wr_lo_ref, br_ref, route_ref, counts_ref, base_ref)


def _fft2_ln_kernel(a_ref, g_tab_ref, mr_ref, mi_ref, bo_ref, x_ref, g_ref, b_ref,
                    wr_hi_ref, wr_lo_ref, br_ref,
                    xt_ref, route_ref, counts_ref, base_ref):
    @pl.when(pl.program_id(0) == 0)
    def _():
        base_ref[...] = jnp.zeros(base_ref.shape, F32)

    rows = xt_ref.shape[0]
    d = xt_ref.shape[1] * LANES
    kb = g_tab_ref.shape[0]
    s1 = rows // kb
    at = pltpu.einshape("njd->jnd", a_ref[...])
    vr, vi = [], []
    for j in range(kb):
        aj = jnp.concatenate([at[j], at[kb + j]], axis=0)
        vj = jnp.dot(g_tab_ref[j], aj, preferred_element_type=F32)
        vr.append(vj[:s1])
        vi.append(vj[s1:])
    vr = pltpu.einshape("jkd->kjd", jnp.stack(vr)).reshape(rows, d)
    vi = pltpu.einshape("jkd->kjd", jnp.stack(vi)).reshape(rows, d)
    h = (jnp.dot(vr.astype(BF16), mr_ref[...], preferred_element_type=F32)
         + jnp.dot(vi.astype(BF16), mi_ref[...], preferred_element_type=F32)
         + bo_ref[...])
    xn = _ln(ALPHA * x_ref[...].reshape(rows, d) + h, g_ref[...], b_ref[...])
    xt_ref[...] = _tiles_from_rows(xn)
    _route_epilogue(xn, wr_hi_ref, wr_lo_ref, br_ref, route_ref, counts_ref, base_ref)


def _router_specs(d, rows):
    const2 = lambda i: (0, 0)
    ins = [pl.BlockSpec((1, d), const2), pl.BlockSpec((1, d), const2),
           pl.BlockSpec((ROUTER_ROWS, d), const2), pl.BlockSpec((ROUTER_ROWS, d), const2),
           pl.BlockSpec((ROUTER_ROWS, 1), const2)]
    outs = [pl.BlockSpec((rows, d // LANES, LANES), lambda i: (i, 0, 0)),
            pl.BlockSpec((SUBLANES, rows), lambda i: (0, i)),
            pl.BlockSpec((N_EXPERTS, LANES), const2)]
    return ins, outs


def _router_out_shapes(t, d):
    return (jax.ShapeDtypeStruct((t, d // LANES, LANES), F32),
            jax.ShapeDtypeStruct((SUBLANES, t), F32),
            jax.ShapeDtypeStruct((N_EXPERTS, LANES), F32))


def _proj_ln_router(att, wo_bf16, x2, ln_g, ln_b, router):
    t, d = x2.shape
    rows = ROW_TILE
    tail_in, outs = _router_specs(d, rows)
    return pl.pallas_call(
        _proj_ln_kernel,
        out_shape=_router_out_shapes(t, d),
        grid=(t // rows,),
        in_specs=[pl.BlockSpec((rows, d), lambda i: (i, 0)),
                  pl.BlockSpec((d, d), lambda i: (0, 0)),
                  pl.BlockSpec((rows, d), lambda i: (i, 0))] + tail_in,
        out_specs=outs,
        scratch_shapes=[pltpu.VMEM((N_EXPERTS, 1), F32)],
        compiler_params=_cparams("arbitrary"),
        name="proj_ln_router",
    )(att, wo_bf16, x2, ln_g, ln_b, *router)


def _fft1_kernel(f_ref, x_ref, o_ref):
    xt = pltpu.einshape("njd->jnd", x_ref[...])
    for j in range(xt.shape[0]):
        o_ref[j] = jnp.dot(f_ref[...], xt[j].astype(BF16), preferred_element_type=F32).astype(BF16)


def _fft_stage1(x3):
    b, s, d = x3.shape
    s1 = FFT_S1
    s2 = s // s1
    kb = FFT_KB
    k = np.arange(s2)
    ang = 2.0 * np.pi * ((k[:, None] * k[None, :]) % s2) / s2
    f = np.stack([np.cos(ang), -np.sin(ang)])
    f = f.reshape(2, s2 // kb, kb, s2).transpose(1, 0, 2, 3).reshape(2 * s2, s2)
    f = jnp.asarray(f, dtype=BF16)
    n1_tile = SUBLANES
    return pl.pallas_call(
        _fft1_kernel,
        out_shape=jax.ShapeDtypeStruct((b, s1, 2 * s2, d), BF16),
        grid=(b, s1 // n1_tile),
        in_specs=[pl.BlockSpec((2 * s2, s2), lambda bi, ni: (0, 0)),
                  pl.BlockSpec((None, s2, n1_tile, d), lambda bi, ni: (bi, 0, ni, 0))],
        out_specs=pl.BlockSpec((None, n1_tile, 2 * s2, d), lambda bi, ni: (bi, ni, 0, 0)),
        compiler_params=_cparams("parallel", "parallel"),
        name="fft_stage1",
    )(f, x3.reshape(b, s2, s1, d))


def _fft_stage2_table(s):
    s1 = FFT_S1
    s2 = s // s1
    k2 = np.arange(s2)[:, None, None]
    k1 = np.arange(s1)[None, :, None]
    n1 = np.arange(s1)[None, None, :]
    ang = 2.0 * np.pi * ((k1 * n1 * s2 + k2 * n1) % s) / s
    gr, gi = np.cos(ang), -np.sin(ang)
    g = np.concatenate([np.concatenate([gr, -gi], axis=2), np.concatenate([gi, gr], axis=2)], axis=1)
    return jnp.asarray(g, dtype=BF16)


def _fold_kernel(cs_hi_ref, cs_lo_ref, w_ref, o_ref):
    w_hi, w_lo = _split_bf16(w_ref[...])
    o_ref[...] = (jnp.dot(cs_hi_ref[...], w_hi, preferred_element_type=F32)
                  + jnp.dot(cs_hi_ref[...], w_lo, preferred_element_type=F32)
                  + jnp.dot(cs_lo_ref[...], w_hi, preferred_element_type=F32)).astype(BF16)


def _fold_channel_dft(w_o, s):
    d = w_o.shape[0]
    cg = d // FNET_GROUPS
    c = np.arange(cg)
    ang = 2.0 * np.pi * ((c[:, None] * c[None, :]) % cg) / cg
    scale = 1.0 / math.sqrt(s * cg)
    cs = jnp.asarray(np.stack([np.cos(ang), np.sin(ang)]) * scale, dtype=F32)
    cs_hi, cs_lo = _split_bf16(cs)
    out = pl.pallas_call(
        _fold_kernel,
        out_shape=jax.ShapeDtypeStruct((2, d, d), BF16),
        grid=(2, FNET_GROUPS),
        in_specs=[pl.BlockSpec((None, cg, cg), lambda ci, gi: (ci, 0, 0)),
                  pl.BlockSpec((None, cg, cg), lambda ci, gi: (ci, 0, 0)),
                  pl.BlockSpec((cg, d), lambda ci, gi: (gi, 0))],
        out_specs=pl.BlockSpec((None, cg, d), lambda ci, gi: (ci, gi, 0)),
        compiler_params=_cparams("parallel", "parallel"),
        name="fold_channel_dft",
    )(cs_hi, cs_lo, w_o.astype(F32))
    return out[0], out[1]


def _fft2_ln_router(a1, gbig, mr, mi, b_o, x3, ln_g, ln_b, router):
    b, s, d = x3.shape
    s1 = FFT_S1
    s2 = s // s1
    kb = FFT_KB
    nblk = s2 // kb
    rows = s1 * kb
    t = b * s
    a5 = a1.reshape(b, s1, nblk, 2 * kb, d)
    x5 = x3.reshape(b, s1, nblk, kb, d)
    tail_in, outs = _router_specs(d, rows)
    return pl.pallas_call(
        _fft2_ln_kernel,
        out_shape=_router_out_shapes(t, d),
        grid=(nblk * b,),
        in_specs=[pl.BlockSpec((None, s1, None, 2 * kb, d), lambda i: (i % b, 0, i // b, 0, 0)),
                  pl.BlockSpec((kb, 2 * s1, 2 * s1), lambda i: (i // b, 0, 0)),
                  pl.BlockSpec((d, d), lambda i: (0, 0)),
                  pl.BlockSpec((d, d), lambda i: (0, 0)),
                  pl.BlockSpec((1, d), lambda i: (0, 0)),
                  pl.BlockSpec((None, s1, None, kb, d), lambda i: (i % b, 0, i // b, 0, 0))] + tail_in,
        out_specs=outs,
        scratch_shapes=[pltpu.VMEM((N_EXPERTS, 1), F32)],
        compiler_params=_cparams("arbitrary"),
        name="fft2_ln_router",
    )(a5, gbig, mr, mi, b_o, x5, ln_g, ln_b, *router)


def _rows_from_tiles(ref, *lead):
    xt = pltpu.einshape("rcl->crl", ref[lead] if lead else ref[...])
    return jnp.concatenate([xt[c] for c in range(xt.shape[0])], axis=1)


def _tiles_from_rows(y):
    yt = jnp.stack([y[:, c * LANES:(c + 1) * LANES] for c in range(y.shape[1] // LANES)])
    return pltpu.einshape("crl->rcl", yt)


def _expert_kernel(bexp_ref, nused_ref, src_ref, src_next_ref, dst_ref, xn_ref, w1_ref, w3_ref, w2_ref, out_ref,
                   xbuf, ybuf, w13_bf, w2_bf, gsem, ssem, *, de):
    i = pl.program_id(0)
    last = pl.num_programs(0) - 1
    n_used = nused_ref[0]
    slot = lax.rem(i, 2)
    blk = xbuf.shape[1]
    half = blk // 2

    def gather(idx_ref, sl):
        def body(r, carry):
            for pr in range(2):
                row = pr * half + r
                pltpu.make_async_copy(xn_ref.at[idx_ref[0, 0, row]], xbuf.at[sl, row],
                                      gsem.at[sl]).start(priority=pr)
            return carry
        lax.fori_loop(0, half, body, 0, unroll=8)

    def gather_wait(sl):
        pltpu.make_async_copy(xn_ref.at[pl.ds(0, blk)], xbuf.at[sl], gsem.at[sl]).wait()

    def scatter_wait(sl):
        pltpu.make_async_copy(ybuf.at[sl], out_ref.at[pl.ds(0, blk)], ssem.at[sl]).wait()

    @pl.when(i == 0)
    def _():
        gather(src_ref, 0)

    @pl.when(i + 1 < n_used)
    def _():
        gather(src_next_ref, 1 - slot)

    @pl.when((i == 0) | (bexp_ref[i] != bexp_ref[jnp.maximum(i - 1, 0)]))
    def _():
        w13_bf[:, :de] = w1_ref[...].astype(BF16)
        w13_bf[:, de:] = w3_ref[...].astype(BF16)
        w2_bf[...] = w2_ref[...].astype(BF16)

    @pl.when(i >= 2)
    def _():
        scatter_wait(slot)

    @pl.when(i < n_used)
    def _():
        gather_wait(slot)
        xb = _rows_from_tiles(xbuf, slot).astype(BF16)
        h = jnp.dot(xb, w13_bf[...], preferred_element_type=F32)
        a = h[:, :de]
        act = (a / (1.0 + jnp.exp(-a))) * h[:, de:]
        y = jnp.dot(act.astype(BF16), w2_bf[...], preferred_element_type=F32)
        ybuf[slot] = _tiles_from_rows(y)

        def body(r, carry):
            for pr in range(2):
                row = pr * half + r
                pltpu.make_async_copy(ybuf.at[slot, row], out_ref.at[dst_ref[0, 0, row]],
                                      ssem.at[slot]).start(priority=pr)
            return carry
        lax.fori_loop(0, half, body, 0, unroll=8)

    @pl.when(i >= n_used)
    def _():
        ybuf[slot] = jnp.zeros(ybuf.shape[1:], F32)
        pltpu.make_async_copy(ybuf.at[slot], out_ref.at[pl.ds(dst_ref[0, 0, 0], blk)], ssem.at[slot]).start()

    @pl.when(i == last)
    def _():
        @pl.when(i >= 1)
        def _():
            scatter_wait(1 - slot)
        scatter_wait(slot)


def _expert_mlp(xn_tiles, src_tok, dst_row, layer, w1, w3, w2, block_exp, n_used):
    nb, _, blk = src_tok.shape
    tile_rows = xn_tiles.shape[1]
    d = tile_rows * LANES
    de = w2.shape[2]

    def w_map(i, bexp, nused):
        return (layer, bexp[i], 0, 0)

    smem_block = functools.partial(pl.BlockSpec, (1, 1, blk), memory_space=pltpu.SMEM)
    grid_spec = pltpu.PrefetchScalarGridSpec(
        num_scalar_prefetch=2,
        grid=(nb,),
        in_specs=[smem_block(lambda i, bexp, nused: (i, 0, 0)),
                  smem_block(lambda i, bexp, nused: (jnp.minimum(i + 1, nb - 1), 0, 0)),
                  smem_block(lambda i, bexp, nused: (i, 0, 0)),
                  pl.BlockSpec(memory_space=pl.ANY),
                  pl.BlockSpec((None, None, d, de), w_map),
                  pl.BlockSpec((None, None, d, de), w_map),
                  pl.BlockSpec((None, None, de, d), w_map)],
        out_specs=pl.BlockSpec(memory_space=pl.ANY),
        scratch_shapes=[pltpu.VMEM((2, blk, tile_rows, LANES), F32), pltpu.VMEM((2, blk, tile_rows, LANES), F32),
                        pltpu.VMEM((d, 2 * de), BF16), pltpu.VMEM((de, d), BF16),
                        pltpu.SemaphoreType.DMA((2,)), pltpu.SemaphoreType.DMA((2,))],
    )
    return pl.pallas_call(
        functools.partial(_expert_kernel, de=de),
        out_shape=jax.ShapeDtypeStruct((nb * blk, tile_rows, LANES), F32),
        grid_spec=grid_spec,
        compiler_params=_cparams("arbitrary"),
        name="expert_mlp",
    )(block_exp, n_used, src_tok, src_tok, dst_row, xn_tiles, w1, w3, w2)


def _combine_ln_kernel(y1_ref, y2_ref, x_ref, gates_ref, g_ref, b_ref, o_ref):
    gates = gates_ref[...]
    m = gates[:, 0:1] * _rows_from_tiles(y1_ref) + gates[:, 1:2] * _rows_from_tiles(y2_ref)
    y = _ln(ALPHA * _rows_from_tiles(x_ref) + m, g_ref[...], b_ref[...])
    o_ref[...] = y.reshape(o_ref.shape)


def _combine_ln(ys, xn_tiles, gates, ln_g, ln_b, out_shape, out_spec):
    t, tile_rows, _ = xn_tiles.shape
    d = tile_rows * LANES
    rows = ROW_TILE
    nsteps = t // rows
    return pl.pallas_call(
        _combine_ln_kernel,
        out_shape=out_shape,
        grid=(nsteps,),
        in_specs=[pl.BlockSpec((rows, tile_rows, LANES), lambda i: (i, 0, 0)),
                  pl.BlockSpec((rows, tile_rows, LANES), lambda i: (nsteps + i, 0, 0)),
                  pl.BlockSpec((rows, tile_rows, LANES), lambda i: (i, 0, 0)),
                  pl.BlockSpec((rows, 2), lambda i: (i, 0)),
                  pl.BlockSpec((1, d), lambda i: (0, 0)),
                  pl.BlockSpec((1, d), lambda i: (0, 0))],
        out_specs=out_spec,
        compiler_params=_cparams("parallel"),
        name="moe_combine_ln",
    )(ys, ys, xn_tiles, gates, ln_g, ln_b)


def _moe_ln(xn_tiles, route, counts, experts_w, ln_g, ln_b, out_shape, out_spec):
    t = xn_tiles.shape[0]
    blk = EXPERT_BLOCK
    nb = (2 * t) // blk + N_EXPERTS
    p_rows = nb * blk
    n_dummy = p_rows - 2 * t
    eid = route[0:2].astype(jnp.int32).reshape(2 * t)
    cnt = counts[:, 0].astype(jnp.int32)
    padded = (cnt + blk - 1) // blk * blk
    pends = jnp.cumsum(padded)
    pad_ends = jnp.cumsum(padded - cnt)
    dummy = jnp.arange(n_dummy, dtype=jnp.int32) - (p_rows - pends[-1])
    dummy_expert = jnp.where(dummy < 0, N_EXPERTS,
                             jnp.sum((pad_ends[None, :] <= dummy[:, None]).astype(jnp.int32), axis=1))
    group = jnp.concatenate([2 * eid, 2 * dummy_expert + 1])
    slot_asg = lax.sort(group * p_rows + jnp.arange(p_rows, dtype=jnp.int32)) % p_rows
    src_tok = jnp.where(slot_asg < t, slot_asg, jnp.where(slot_asg < 2 * t, slot_asg - t, 0))
    block_row0 = jnp.arange(nb, dtype=jnp.int32)[:, None] * blk
    block_exp = jnp.minimum(jnp.sum((pends[None, :] <= block_row0).astype(jnp.int32), axis=1), N_EXPERTS - 1)
    n_used = (pends[-1:] // blk).astype(jnp.int32)
    gates = jnp.transpose(route[2:4])

    ys = _expert_mlp(xn_tiles, src_tok.reshape(nb, 1, blk), slot_asg.reshape(nb, 1, blk), *experts_w,
                     block_exp, n_used)
    return _combine_ln(ys, xn_tiles, gates, ln_g, ln_b, out_shape, out_spec)


def _router_params(w_group, b_group, w_fine, b_fine):
    d = w_group.shape[0]
    w = jnp.zeros((ROUTER_ROWS, d), F32)
    w = w.at[0:N_GROUPS].set(jnp.transpose(w_group).astype(F32))
    w = w.at[FINE_ROW0:FINE_ROW0 + N_EXPERTS].set(jnp.transpose(w_fine).astype(F32))
    bias = jnp.zeros((ROUTER_ROWS, 1), F32).at[N_GROUPS:FINE_ROW0, 0].set(NEG_BIG)
    bias = bias.at[0:N_GROUPS, 0].set(b_group.astype(F32))
    bias = bias.at[FINE_ROW0:FINE_ROW0 + N_EXPERTS, 0].set(b_fine.astype(F32))
    w_hi, w_lo = _split_bf16(w)
    return w_hi, w_lo, bias


def _row(v):
    return v.reshape(1, -1).astype(F32)


def _trunk(x, p, shared):
    b, s, d = x.shape
    t = b * s
    assert s % ATT_TILE == 0 and t % ROW_TILE == 0 and s % (FFT_S1 * FFT_KB) == 0
    assert ROW_TILE == FFT_S1 * FFT_KB
    flat_shape = jax.ShapeDtypeStruct((t, d), F32)
    flat_spec = pl.BlockSpec((ROW_TILE, d), lambda i: (i, 0))
    x2 = x.reshape(t, d)
    for i in range(DEPTH):
        j = i // 2
        if i % 2 == 0:
            qkv = _qkv_proj(x2, shared["wqkv"][j])
            att = _attention(qkv, *shared["bias"], shared["lam"][j], shared["subln_g"][j], b, s, d,
                             shared["lam_init"][j])
            xt, route, counts = _proj_ln_router(att, shared["wo"][j], x2, _row(p["ln1_g"][i]),
                                                _row(p["ln1_b"][i]), shared["router"][i])
            x2 = _moe_ln(xt, route, counts, shared["experts"][i],
                         _row(p["ln2_g"][i]), _row(p["ln2_b"][i]), flat_shape, flat_spec)
        else:
            x3 = x2.reshape(b, s, d)
            a1 = _fft_stage1(x3)
            mr, mi = shared["fold"][(j, s)]
            xt, route, counts = _fft2_ln_router(a1, shared["gbig"][s], mr, mi, _row(p["fnet_b_o"][j]), x3,
                                                _row(p["ln1_g"][i]), _row(p["ln1_b"][i]), shared["router"][i])
            nblk = s // (FFT_S1 * FFT_KB)
            out_shape = jax.ShapeDtypeStruct((b, FFT_S1, nblk, FFT_KB, d), F32)
            out_spec = pl.BlockSpec((None, FFT_S1, None, FFT_KB, d), lambda i2: (i2 % b, 0, i2 // b, 0, 0))
            y5 = _moe_ln(xt, route, counts, shared["experts"][i],
                         _row(p["ln2_g"][i]), _row(p["ln2_b"][i]), out_shape, out_spec)
            x2 = y5.reshape(t, d)
    return x2.reshape(b, s, d)


def kernel(x_prompt, x_sample, rel_bias, attn_w_qkv, attn_lambda_q1, attn_lambda_k1, attn_lambda_q2, attn_lambda_k2, attn_subln_g, attn_w_o, fnet_w_o, fnet_b_o, ln1_g, ln1_b, ln2_g, ln2_b, moe_w_group, moe_b_group, moe_w_fine, moe_b_fine, moe_w1, moe_w3, moe_w2):
    d = x_prompt.shape[-1]
    p = dict(fnet_b_o=fnet_b_o, ln1_g=ln1_g, ln1_b=ln1_b, ln2_g=ln2_g, ln2_b=ln2_b)
    n_attn = attn_w_qkv.shape[0]
    n_fnet = fnet_w_o.shape[0]
    seqs = sorted({x_prompt.shape[1], x_sample.shape[1]})

    qscale = jnp.concatenate([jnp.full((d,), HEAD_DIM ** -0.5 * LOG2E, F32), jnp.ones((2 * d,), F32)])
    shared = dict(
        wqkv=[(attn_w_qkv[j].astype(F32) * qscale).astype(BF16) for j in range(n_attn)],
        wo=[attn_w_o[j].astype(BF16) for j in range(n_attn)],
        subln_g=[_row(attn_subln_g[j]) for j in range(n_attn)],
        lam_init=[0.8 - 0.6 * math.exp(-0.3 * (2 * j)) for j in range(n_attn)],
        bias=_bias_tables(rel_bias, ATT_TILE),
        router=[_router_params(moe_w_group[i], moe_b_group[i], moe_w_fine[i], moe_b_fine[i])
                for i in range(DEPTH)],
        experts=[(i, moe_w1, moe_w3, moe_w2) for i in range(DEPTH)],
        gbig={s: _fft_stage2_table(s) for s in seqs},
        fold={(j, s): _fold_channel_dft(fnet_w_o[j], s) for j in range(n_fnet) for s in seqs},
    )
    shared["lam"] = [
        (jnp.exp(jnp.sum(attn_lambda_q1[j].astype(F32) * attn_lambda_k1[j].astype(F32)))
         - jnp.exp(jnp.sum(attn_lambda_q2[j].astype(F32) * attn_lambda_k2[j].astype(F32)))
         + shared["lam_init"][j]).reshape(1).astype(F32)
        for j in range(n_attn)]

    return (_trunk(x_prompt, p, shared), _trunk(x_sample, p, shared))
```
